```python
import jax, jax.numpy as jnp
from jax import lax
import numpy as np

D_MODEL = 1024
BATCH = 8
SEQ = 2048
DEPTH = 4
DEC_BATCH = 32
DEC_SEQ = 1
PAST_LEN = 8192
PAGE_SIZE = 128

N_MIXERS = 2
N_SB_LAYERS = (DEPTH + 1) // 2
N_ML_LAYERS = DEPTH // 2
SB_HEADS = 16
SB_HEAD_DIM = D_MODEL // SB_HEADS
SB_BLOCK = 128
SB_MIN_LOG2_SPAN = 4.0
SB_MAX_LOG2_SPAN = 16.0
ML_HEADS = 8
ML_DV = D_MODEL // ML_HEADS
ML_DQK = ML_DV // 2
ML_CHUNK = 64
D_FF = ((8 * D_MODEL // 3 + 127) // 128) * 128
PLE_DIM = 256
RMS_EPS = 1e-6
FORGET_BIAS = 3.0

kernel_name = "stickbreak_mlstm_macaron_hybrid_step"


def rms_norm(x, g):
    xf = x.astype(jnp.float32)
    y = xf * lax.rsqrt(jnp.mean(xf * xf, axis=-1, keepdims=True) + RMS_EPS)
    return (y * g.astype(jnp.float32)).astype(x.dtype)


def swiglu_ffn(h, w_in, w_out):
    gate, up = jnp.split(h @ w_in, 2, axis=-1)
    return (jax.nn.silu(gate) * up) @ w_out


def stick_breaking_block(q, k, v, bias, q_pos, k_pos):
    z = jnp.einsum("bqhd,bkhd->bhqk", q, k).astype(jnp.float32) * (SB_HEAD_DIM ** -0.5)
    z = z + bias.astype(jnp.float32)[None, :, None, None]
    mask = k_pos[None, :] < q_pos[:, None]
    log_1m = jnp.where(mask, jax.nn.log_sigmoid(-z), 0.0)
    between = lax.cumsum(log_1m, axis=3, reverse=True) - log_1m
    a = jnp.where(mask, jnp.exp(jax.nn.log_sigmoid(z) + between), 0.0)
    return jnp.einsum("bhqk,bkhd->bqhd", a.astype(v.dtype), v)


def sb_qkv(h, w_qkv):
    B, S, _ = h.shape
    q, k, v = jnp.split(h @ w_qkv, 3, axis=-1)
    shp = (B, S, SB_HEADS, SB_HEAD_DIM)
    return q.reshape(shp), k.reshape(shp), v.reshape(shp)


def sb_prompt(h, w_qkv, w_o, bias):
    B, S, _ = h.shape
    q, k, v = sb_qkv(h, w_qkv)
    nb = S // SB_BLOCK
    q_blocks = jnp.moveaxis(q.reshape(B, nb, SB_BLOCK, SB_HEADS, SB_HEAD_DIM), 1, 0)
    q_pos = jnp.arange(S, dtype=jnp.int32).reshape(nb, SB_BLOCK)
    k_pos = jnp.arange(S, dtype=jnp.int32)
    o = lax.map(lambda blk: stick_breaking_block(blk[0], k, v, bias, blk[1], k_pos), (q_blocks, q_pos))
    o = jnp.moveaxis(o, 0, 1).reshape(B, S, SB_HEADS * SB_HEAD_DIM)
    return o @ w_o, k, v


def sb_sample(h, k_pool, v_pool, page_table, w_qkv, w_o, bias):
    B, T, _ = h.shape
    q, k, v = sb_qkv(h, w_qkv)
    past = page_table.shape[1] * k_pool.shape[1]
    k_past = k_pool[page_table].reshape(B, past, SB_HEADS, SB_HEAD_DIM).astype(k.dtype)
    v_past = v_pool[page_table].reshape(B, past, SB_HEADS, SB_HEAD_DIM).astype(v.dtype)
    keys = jnp.concatenate([k_past, k], axis=1)
    vals = jnp.concatenate([v_past, v], axis=1)
    q_pos = past + jnp.arange(T, dtype=jnp.int32)
    k_pos = jnp.arange(past + T, dtype=jnp.int32)
    o = stick_breaking_block(q, keys, vals, bias, q_pos, k_pos).reshape(B, T, SB_HEADS * SB_HEAD_DIM)
    return o @ w_o, k, v


def mlstm_chunkwise(q, k, v, ig, lf, C0, n0, m0):
    B, S, H, _ = q.shape
    L = ML_CHUNK if S % ML_CHUNK == 0 else S
    nc = S // L

    def chunks(a):
        return jnp.moveaxis(a.reshape((B, nc, L) + a.shape[2:]), 1, 0)

    causal = jnp.tril(jnp.ones((L, L), dtype=bool))

    def step(carry, inp):
        C, n, m = carry
        qc, kc, vc, ic, fc = inp
        b = jnp.cumsum(fc, axis=1)
        d = jnp.where(causal[None, :, :, None],
                      b[:, :, None, :] - b[:, None, :, :] + ic[:, None, :, :], -jnp.inf)
        inter = b + m[:, None, :]
        m_t = jnp.maximum(inter, jnp.max(d, axis=2))
        w = jnp.exp(d - m_t[:, :, None, :]) * jnp.einsum("bthd,bshd->btsh", qc, kc)
        g = jnp.exp(inter - m_t)
        num = jnp.einsum("btsh,bshe->bthe", w, vc) + g[..., None] * jnp.einsum("bthd,bhde->bthe", qc, C)
        den = jnp.sum(w, axis=2) + g * jnp.einsum("bthd,bhd->bth", qc, n)
        h = num / jnp.maximum(jnp.abs(den), jnp.exp(-m_t))[..., None]
        m_new = m_t[:, -1]
        wk = jnp.exp(b[:, -1:, :] - b + ic - m_new[:, None, :])
        g_end = jnp.exp(b[:, -1] + m - m_new)
        C_new = g_end[..., None, None] * C + jnp.einsum("bsh,bshd,bshe->bhde", wk, kc, vc)
        n_new = g_end[..., None] * n + jnp.einsum("bsh,bshd->bhd", wk, kc)
        return (C_new, n_new, m_new), h

    (C, n, m), h = lax.scan(step, (C0, n0, m0), (chunks(q), chunks(k), chunks(v), chunks(ig), chunks(lf)))
    return jnp.moveaxis(h, 0, 1).reshape(B, S, H, ML_DV), C, n, m


def mlstm_mixer(h, w_in, b_gates, head_g, w_out, C0, n0, m0):
    B, S, _ = h.shape
    f32 = jnp.float32
    hq, hv = ML_HEADS * ML_DQK, ML_HEADS * ML_DV
    q, k, v, o, gates = jnp.split(h @ w_in, [hq, 2 * hq, 2 * hq + hv, 2 * hq + 2 * hv], axis=-1)
    q = q.astype(f32).reshape(B, S, ML_HEADS, ML_DQK) * (ML_DQK ** -0.5)
    k = k.astype(f32).reshape(B, S, ML_HEADS, ML_DQK)
    v = v.astype(f32).reshape(B, S, ML_HEADS, ML_DV)
    gates = gates.astype(f32) + b_gates.astype(f32)
    ig = gates[..., :ML_HEADS]
    lf = jax.nn.log_sigmoid(gates[..., ML_HEADS:])
    hs, C, n, m = mlstm_chunkwise(q, k, v, ig, lf, C0.astype(f32), n0.astype(f32), m0.astype(f32))
    hs = rms_norm(hs, head_g).reshape(B, S, hv)
    out = (jax.nn.sigmoid(o.astype(f32)) * hs).astype(h.dtype) @ w_out
    return out, C, n, m


def setup_inputs(seed: int = 0) -> dict:
    key = jax.random.key(seed)
    ks = jax.random.split(key, 24)
    f32 = jnp.float32
    n_pages = PAST_LEN // PAGE_SIZE
    n_used = DEC_BATCH * n_pages
    n_phys = (5 * n_used + 3) // 4

    def nrm(k, shape, scale=1.0):
        return jax.random.normal(k, shape, f32) * scale

    hq, hv = ML_HEADS * ML_DQK, ML_HEADS * ML_DV
    sb_w = SB_HEADS * SB_HEAD_DIM
    log_span = jnp.log(2.0) * jnp.linspace(SB_MIN_LOG2_SPAN, SB_MAX_LOG2_SPAN, SB_HEADS, dtype=f32)
    return {
        "x_prompt": nrm(ks[0], (BATCH, SEQ, D_MODEL)),
        "x_sample": nrm(ks[1], (DEC_BATCH, DEC_SEQ, D_MODEL)),
        "cache_k": nrm(ks[2], (N_SB_LAYERS, n_phys, PAGE_SIZE, SB_HEADS, SB_HEAD_DIM)),
        "cache_v": nrm(ks[3], (N_SB_LAYERS, n_phys, PAGE_SIZE, SB_HEADS, SB_HEAD_DIM)),
        "state_C": nrm(ks[4], (N_ML_LAYERS, DEC_BATCH, ML_HEADS, ML_DQK, ML_DV), 0.5),
        "state_n": nrm(ks[5], (N_ML_LAYERS, DEC_BATCH, ML_HEADS, ML_DQK), 0.5),
        "state_m": nrm(ks[6], (N_ML_LAYERS, DEC_BATCH, ML_HEADS), 0.5),
        "page_table": jax.random.permutation(ks[7], n_phys)[:n_used].reshape(DEC_BATCH, n_pages).astype(jnp.int32),
        "p_prompt": nrm(ks[8], (DEPTH, BATCH, SEQ, PLE_DIM)),
        "p_sample": nrm(ks[9], (DEPTH, DEC_BATCH, DEC_SEQ, PLE_DIM)),
        "norm_g": 1.0 + nrm(ks[10], (DEPTH, 4, D_MODEL), 0.02),
        "ffn_w_in": nrm(ks[11], (DEPTH, 2, D_MODEL, 2 * D_FF), D_MODEL ** -0.5),
        "ffn_w_out": nrm(ks[12], (DEPTH, 2, D_FF, D_MODEL), D_FF ** -0.5),
        "sb_w_qkv": nrm(ks[13], (N_SB_LAYERS, D_MODEL, 3 * sb_w), D_MODEL ** -0.5),
        "sb_w_o": nrm(ks[14], (N_SB_LAYERS, sb_w, D_MODEL), sb_w ** -0.5),
        "sb_logit_bias": -log_span[None, :] + nrm(ks[23], (N_SB_LAYERS, SB_HEADS), 0.1),
        "ml_w_in": nrm(ks[15], (N_ML_LAYERS, D_MODEL, 2 * hq + 2 * hv + 2 * ML_HEADS), D_MODEL ** -0.5),
        "ml_b_gates": jnp.concatenate([nrm(ks[16], (N_ML_LAYERS, ML_HEADS), 0.1),
                                       FORGET_BIAS + nrm(ks[17], (N_ML_LAYERS, ML_HEADS), 0.1)], axis=-1),
        "ml_head_g": 1.0 + nrm(ks[18], (N_ML_LAYERS, ML_HEADS, ML_DV), 0.02),
        "ml_w_out": nrm(ks[19], (N_ML_LAYERS, hv, D_MODEL), hv ** -0.5),
        "ple_w_proj": nrm(ks[20], (DEPTH, PLE_DIM, D_MODEL), PLE_DIM ** -0.5),
        "ple_w_gate": nrm(ks[21], (DEPTH, D_MODEL, D_MODEL), D_MODEL ** -0.5),
        "final_norm_g": 1.0 + nrm(ks[22], (D_MODEL,), 0.02),
    }


def reference(x_prompt, x_sample, cache_k, cache_v, state_C, state_n, state_m, page_table,
              p_prompt, p_sample, norm_g, ffn_w_in, ffn_w_out, sb_w_qkv, sb_w_o, sb_logit_bias,
              ml_w_in, ml_b_gates, ml_head_g, ml_w_out, ple_w_proj, ple_w_gate, final_norm_g):
    yp, ys = x_prompt, x_sample
    bp = x_prompt.shape[0]
    f32 = jnp.float32
    kp_l, vp_l, ks_l, vs_l = [], [], [], []
    Cp_l, np_l, mp_l, Cs_l, ns_l, ms_l = [], [], [], [], [], []
    for i in range(DEPTH):
        g = norm_g[i]
        yp = yp + 0.5 * swiglu_ffn(rms_norm(yp, g[0]), ffn_w_in[i, 0], ffn_w_out[i, 0])
        ys = ys + 0.5 * swiglu_ffn(rms_norm(ys, g[0]), ffn_w_in[i, 0], ffn_w_out[i, 0])
        hp, hs = rms_norm(yp, g[1]), rms_norm(ys, g[1])
        j = i // N_MIXERS
        if i % N_MIXERS == 0:
            op, kp, vp = sb_prompt(hp, sb_w_qkv[j], sb_w_o[j], sb_logit_bias[j])
            osm, ksm, vsm = sb_sample(hs, cache_k[j], cache_v[j], page_table, sb_w_qkv[j], sb_w_o[j],
                                      sb_logit_bias[j])
            kp_l.append(kp); vp_l.append(vp); ks_l.append(ksm); vs_l.append(vsm)
        else:
            C0 = jnp.zeros((bp, ML_HEADS, ML_DQK, ML_DV), f32)
            n0 = jnp.zeros((bp, ML_HEADS, ML_DQK), f32)
            m0 = jnp.zeros((bp, ML_HEADS), f32)
            op, Cp, npp, mp = mlstm_mixer(hp, ml_w_in[j], ml_b_gates[j], ml_head_g[j], ml_w_out[j], C0, n0, m0)
            osm, Cs, nss, mss = mlstm_mixer(hs, ml_w_in[j], ml_b_gates[j], ml_head_g[j], ml_w_out[j],
                                            state_C[j], state_n[j], state_m[j])
            Cp_l.append(Cp); np_l.append(npp); mp_l.append(mp)
            Cs_l.append(Cs); ns_l.append(nss); ms_l.append(mss)
        yp = yp + op
        ys = ys + osm
        yp = yp + 0.5 * swiglu_ffn(rms_norm(yp, g[2]), ffn_w_in[i, 1], ffn_w_out[i, 1])
        ys = ys + 0.5 * swiglu_ffn(rms_norm(ys, g[2]), ffn_w_in[i, 1], ffn_w_out[i, 1])
        yp = yp + jax.nn.sigmoid(rms_norm(yp, g[3]) @ ple_w_gate[i]) * (p_prompt[i] @ ple_w_proj[i])
        ys = ys + jax.nn.sigmoid(rms_norm(ys, g[3]) @ ple_w_gate[i]) * (p_sample[i] @ ple_w_proj[i])
    yp = rms_norm(yp, final_norm_g)
    ys = rms_norm(ys, final_norm_g)
    return (yp, ys,
            jnp.stack(kp_l), jnp.stack(vp_l), jnp.stack(Cp_l), jnp.stack(np_l), jnp.stack(mp_l),
            jnp.stack(ks_l), jnp.stack(vs_l), jnp.stack(Cs_l), jnp.stack(ns_l), jnp.stack(ms_l))
```

```python
import functools

import jax
import jax.numpy as jnp
from jax import lax
from jax.experimental import pallas as pl
from jax.experimental.pallas import tpu as pltpu

F32 = jnp.float32
BF16 = jnp.bfloat16

RMS_EPS = 1e-6
MXU_COLS = 256
VMEM_LIMIT_BYTES = 56 * 1024 * 1024
ROW_TILE = 512
SB_TILE = 256
ML_TILE = 256
PAGES_PER_STEP = 8


def _cparams(*sem):
    return pltpu.CompilerParams(dimension_semantics=sem, vmem_limit_bytes=VMEM_LIMIT_BYTES)


def _const_spec(shape):
    nd = len(shape)
    return pl.BlockSpec(shape, lambda *_: (0,) * nd, pipeline_mode=pl.Buffered(1))


def _dot(a, b):
    return jnp.dot(a, b, preferred_element_type=F32)


def _rms(x, g):
    return x * lax.rsqrt(jnp.mean(x * x, axis=-1, keepdims=True) + RMS_EPS) * g


def _sigmoid(x):
    return 1.0 / (1.0 + jnp.exp(-x))


def _softplus(z):
    return jnp.maximum(z, 0.0) + jnp.log1p(jnp.exp(-jnp.abs(z)))


def _split_bf16(a, parts):
    out = []
    for _ in range(parts - 1):
        hi = a.astype(BF16)
        out.append(hi)
        a = a - hi.astype(F32)
    out.append(a.astype(BF16))
    return out


def _dot_split_lhs(a, b01, parts):
    acc = None
    for p in _split_bf16(a, parts):
        y = _dot(p, b01)
        acc = y if acc is None else acc + y
    return acc


def _dot_split_rhs(a01, b, parts):
    acc = None
    for p in _split_bf16(b, parts):
        y = _dot(a01, p)
        acc = y if acc is None else acc + y
    return acc


def _col_chunks(n, max_cols):
    assert n % MXU_COLS == 0
    step = max(MXU_COLS, (max_cols // MXU_COLS) * MXU_COLS)
    return [(c, min(c + step, n)) for c in range(0, n, step)]


def _ffn_kernel(x_ref, g_ref, win_ref, wout_ref, o_ref, *, d_ff, chunks):
    x = x_ref[...]
    h = _rms(x, g_ref[...]).astype(BF16)
    acc = None
    for c0, c1 in chunks:
        gate = _dot(h, win_ref[:, c0:c1])
        up = _dot(h, win_ref[:, d_ff + c0:d_ff + c1])
        act = (gate * _sigmoid(gate) * up).astype(BF16)
        y = _dot(act, wout_ref[c0:c1, :])
        acc = y if acc is None else acc + y
    o_ref[...] = x + 0.5 * acc


def _ffn(x, g, w_in, w_out):
    m, d = x.shape
    d_ff = w_out.shape[0]
    tm = min(ROW_TILE, m)
    kern = functools.partial(_ffn_kernel, d_ff=d_ff, chunks=_col_chunks(d_ff, 1536))
    return pl.pallas_call(
        kern, grid=(m // tm,),
        in_specs=[pl.BlockSpec((tm, d), lambda i: (i, 0)),
                  _const_spec((1, d)), _const_spec(w_in.shape), _const_spec(w_out.shape)],
        out_specs=pl.BlockSpec((tm, d), lambda i: (i, 0)),
        out_shape=jax.ShapeDtypeStruct((m, d), F32),
        compiler_params=_cparams("arbitrary"), name="ffn",
    )(x, g, w_in, w_out)


def _rms_matmul_kernel(x_ref, g_ref, w_ref, o_ref):
    h = _rms(x_ref[...], g_ref[...]).astype(BF16)
    o_ref[...] = _dot(h, w_ref[...])


def _rms_matmul(x, g, w):
    m, d = x.shape
    n = w.shape[1]
    return pl.pallas_call(
        _rms_matmul_kernel, grid=(1,),
        in_specs=[_const_spec((m, d)), _const_spec((1, d)), _const_spec(w.shape)],
        out_specs=pl.BlockSpec((m, n), lambda i: (0, 0)),
        out_shape=jax.ShapeDtypeStruct((m, n), F32),
        compiler_params=_cparams("arbitrary"), name="rms_matmul",
    )(x, g, w)


def _head_major_stores(q, kt, v, qb_ref, ktb_ref, vb_ref, *, heads, dqk, dv, tk):
    tm = q.shape[0]
    for h in range(heads):
        qb_ref[h] = q[:, h * dqk:(h + 1) * dqk].astype(BF16)
        vb_ref[h] = v[:, h * dv:(h + 1) * dv].astype(BF16)
        for r in range(tm // tk):
            ktb_ref[h, r] = kt[h * dqk:(h + 1) * dqk, r * tk:(r + 1) * tk].astype(BF16)


def _sb_proj_kernel(x_ref, g_ref, w_ref, kf_ref, vf_ref, qb_ref, ktb_ref, vb_ref, *, heads, hd, tk):
    d = heads * hd
    h = _rms(x_ref[...], g_ref[...]).astype(BF16)
    q = _dot(h, w_ref[:, 0:d]) * (hd ** -0.5)
    k = _dot(h, w_ref[:, d:2 * d])
    v = _dot(h, w_ref[:, 2 * d:3 * d])
    kt = k.T
    kf_ref[0] = kt.reshape(heads, hd, kt.shape[1])
    vf_ref[0] = v.T.reshape(heads, hd, kt.shape[1])
    _head_major_stores(q, kt, v, qb_ref, ktb_ref, vb_ref, heads=heads, dqk=hd, dv=hd, tk=tk)


def _sb_proj(x, g, w, heads, tk, batch):
    m, d = x.shape
    hd = d // heads
    seq = m // batch
    tm = min(ROW_TILE, seq)
    spb = seq // tm
    kv_spec = pl.BlockSpec((1, heads, hd, tm), lambda i: (i // spb, 0, 0, i % spb))
    kern = functools.partial(_sb_proj_kernel, heads=heads, hd=hd, tk=tk)
    return pl.pallas_call(
        kern, grid=(m // tm,),
        in_specs=[pl.BlockSpec((tm, d), lambda i: (i, 0)), _const_spec((1, d)), _const_spec(w.shape)],
        out_specs=[kv_spec, kv_spec,
                   pl.BlockSpec((heads, tm, hd), lambda i: (0, i, 0)),
                   pl.BlockSpec((heads, tm // tk, hd, tk), lambda i: (0, i, 0, 0)),
                   pl.BlockSpec((heads, tm, hd), lambda i: (0, i, 0))],
        out_shape=[jax.ShapeDtypeStruct((batch, heads, hd, seq), F32),
                   jax.ShapeDtypeStruct((batch, heads, hd, seq), F32),
                   jax.ShapeDtypeStruct((heads, m, hd), BF16),
                   jax.ShapeDtypeStruct((heads, m // tk, hd, tk), BF16),
                   jax.ShapeDtypeStruct((heads, m, hd), BF16)],
        compiler_params=_cparams("arbitrary"), name="sb_proj",
    )(x, g, w)


def _ml_proj_kernel(x_ref, g_ref, w_ref, og_ref, gt_ref, qb_ref, ktb_ref, vb_ref,
                    *, heads, dqk, dv, tk):
    hq, hv = heads * dqk, heads * dv
    h = _rms(x_ref[...], g_ref[...]).astype(BF16)
    q = _dot(h, w_ref[:, 0:hq]) * (dqk ** -0.5)
    k = _dot(h, w_ref[:, hq:2 * hq])
    v = _dot(h, w_ref[:, 2 * hq:2 * hq + hv])
    og_ref[...] = _dot(h, w_ref[:, 2 * hq + hv:2 * hq + 2 * hv])
    gt_ref[...] = _dot(h, w_ref[:, 2 * hq + 2 * hv:])
    _head_major_stores(q, k.T, v, qb_ref, ktb_ref, vb_ref, heads=heads, dqk=dqk, dv=dv, tk=tk)


def _ml_proj(x, g, w, heads, dqk, dv, tk):
    m, d = x.shape
    hv = heads * dv
    ng = w.shape[1] - 2 * heads * dqk - 2 * hv
    tm = min(ROW_TILE, m)
    kern = functools.partial(_ml_proj_kernel, heads=heads, dqk=dqk, dv=dv, tk=tk)
    return pl.pallas_call(
        kern, grid=(m // tm,),
        in_specs=[pl.BlockSpec((tm, d), lambda i: (i, 0)), _const_spec((1, d)), _const_spec(w.shape)],
        out_specs=[pl.BlockSpec((tm, hv), lambda i: (i, 0)),
                   pl.BlockSpec((tm, ng), lambda i: (i, 0)),
                   pl.BlockSpec((heads, tm, dqk), lambda i: (0, i, 0)),
                   pl.BlockSpec((heads, tm // tk, dqk, tk), lambda i: (0, i, 0, 0)),
                   pl.BlockSpec((heads, tm, dv), lambda i: (0, i, 0))],
        out_shape=[jax.ShapeDtypeStruct((m, hv), F32), jax.ShapeDtypeStruct((m, ng), F32),
                   jax.ShapeDtypeStruct((heads, m, dqk), BF16),
                   jax.ShapeDtypeStruct((heads, m // tk, dqk, tk), BF16),
                   jax.ShapeDtypeStruct((heads, m, dv), BF16)],
        compiler_params=_cparams("arbitrary"), name="ml_proj",
    )(x, g, w)


def _out_proj_kernel(a_ref, w_ref, x_ref, o_ref):
    o_ref[...] = x_ref[...] + _dot(a_ref[...].astype(BF16), w_ref[...])


def _out_proj(a, w, x):
    m, d = x.shape
    k = a.shape[1]
    tm = min(ROW_TILE, m)
    return pl.pallas_call(
        _out_proj_kernel, grid=(m // tm,),
        in_specs=[pl.BlockSpec((tm, k), lambda i: (i, 0)), _const_spec(w.shape),
                  pl.BlockSpec((tm, d), lambda i: (i, 0))],
        out_specs=pl.BlockSpec((tm, d), lambda i: (i, 0)),
        out_shape=jax.ShapeDtypeStruct((m, d), F32),
        compiler_params=_cparams("arbitrary"), name="out_proj",
    )(a, w, x)


def _ple_kernel(x_ref, p_ref, g_ref, wg_ref, wp_ref, *refs, final):
    x = x_ref[...]
    gate = _sigmoid(_dot(_rms(x, g_ref[...]).astype(BF16), wg_ref[...]))
    out = x + gate * _dot(p_ref[...].astype(BF16), wp_ref[...])
    if final:
        gf_ref, y_ref = refs
        y_ref[...] = _rms(out, gf_ref[...])
    else:
        refs[0][...] = out


def _ple(x, p, g, w_gate, w_proj, g_final=None):
    m, d = x.shape
    dp = p.shape[1]
    tm = min(ROW_TILE, m)
    row = pl.BlockSpec((tm, d), lambda i: (i, 0))
    final = g_final is not None
    extra_specs, extra_args = ([_const_spec((1, d))], [g_final]) if final else ([], [])
    return pl.pallas_call(
        functools.partial(_ple_kernel, final=final), grid=(m // tm,),
        in_specs=[row, pl.BlockSpec((tm, dp), lambda i: (i, 0)), _const_spec((1, d)),
                  _const_spec(w_gate.shape), _const_spec(w_proj.shape)] + extra_specs,
        out_specs=row,
        out_shape=jax.ShapeDtypeStruct((m, d), F32),
        compiler_params=_cparams("arbitrary"), name="ple",
    )(x, p, g, w_gate, w_proj, *extra_args)


def _sb_attn_kernel(bias_ref, q_ref, kt_ref, v_ref, tri_ref, o_ref, rsum_ref, acc_ref,
                    *, tq, heads_per_step):
    hp = pl.program_id(1)
    i = pl.program_id(2)
    tri = tri_ref[...]
    row = lax.broadcasted_iota(jnp.int32, (tq, tq), 0)
    col = lax.broadcasted_iota(jnp.int32, (tq, tq), 1)
    strictly_before = col < row

    def block(j, masked):
        for hh in range(heads_per_step):
            z = _dot(q_ref[hh], kt_ref[hh, j]) + bias_ref[hp * heads_per_step + hh]
            sp = _softplus(z)
            if masked:
                sp = jnp.where(strictly_before, sp, 0.0)
            inc = _dot_split_lhs(sp, tri, 2)
            rsum = rsum_ref[hh]
            a = jnp.exp(z - inc - jnp.concatenate([rsum] * (tq // 128), axis=1))
            if masked:
                a = jnp.where(strictly_before, a, 0.0)
            start = pl.multiple_of(j * tq, tq)
            acc_ref[hh] += _dot(a.astype(BF16), v_ref[hh, pl.ds(start, tq), :])
            rsum_ref[hh] = rsum + jnp.broadcast_to(inc[:, 0:1], rsum.shape)

    rsum_ref[...] = jnp.zeros_like(rsum_ref)
    acc_ref[...] = jnp.zeros_like(acc_ref)
    block(i, True)

    def body(jj, carry):
        block(i - 1 - jj, False)
        return carry

    lax.fori_loop(0, i, body, 0)
    o_ref[...] = jnp.concatenate([acc_ref[hh] for hh in range(heads_per_step)], axis=1).astype(BF16)


def _sb_attn(qb, ktb, vb, bias, tri, batch):
    heads, m, hd = qb.shape
    seq = m // batch
    tq = ktb.shape[3]
    nq = seq // tq
    hps = 128 // hd
    kern = functools.partial(_sb_attn_kernel, tq=tq, heads_per_step=hps)
    return pl.pallas_call(
        kern, grid=(batch, heads // hps, nq),
        in_specs=[pl.BlockSpec(memory_space=pltpu.SMEM),
                  pl.BlockSpec((hps, tq, hd), lambda b, h, i: (h, b * nq + i, 0)),
                  pl.BlockSpec((hps, nq, hd, tq), lambda b, h, i: (h, b, 0, 0)),
                  pl.BlockSpec((hps, seq, hd), lambda b, h, i: (h, b, 0)),
                  _const_spec(tri.shape)],
        out_specs=pl.BlockSpec((tq, hps * hd), lambda b, h, i: (b * nq + i, h)),
        out_shape=jax.ShapeDtypeStruct((m, heads * hd), BF16),
        scratch_shapes=[pltpu.VMEM((hps, tq, 128), F32), pltpu.VMEM((hps, tq, hd), F32)],
        compiler_params=_cparams("arbitrary", "arbitrary", "arbitrary"), name="sb_attn",
    )(bias, qb, ktb, vb, tri)


def _sb_decode_kernel(pt_ref, q_ref, bias_ref, tri_ref, *refs, pages, scale):
    del pt_ref
    k_refs, v_refs = refs[:pages], refs[pages:2 * pages]
    o_ref, qrep_ref, rsum_ref, acc_ref = refs[2 * pages:]
    s = pl.program_id(1)
    heads, hd, page = k_refs[0].shape

    @pl.when(s == 0)
    def _():
        qrow = q_ref[0] * scale
        qrep_ref[...] = jnp.broadcast_to(qrow, (page, heads * hd)).T.reshape(heads, hd, page)
        rsum_ref[...] = jnp.zeros_like(rsum_ref)
        acc_ref[...] = jnp.zeros_like(acc_ref)

    qrep = qrep_ref[...]
    bias = bias_ref[...]
    z = jnp.concatenate([jnp.sum(k_refs[i][...] * qrep, axis=1) + bias for i in range(pages)], axis=0)
    sp = _softplus(z)
    inc = _dot_split_lhs(sp, tri_ref[...], 2)
    rsum = rsum_ref[...]
    for i in reversed(range(pages)):
        rows = slice(i * heads, (i + 1) * heads)
        a = jnp.exp(z[rows] - inc[rows] - rsum)
        rsum = rsum + jnp.broadcast_to(inc[rows, 0:1], rsum.shape)
        for h in range(heads):
            acc_ref[h] += v_refs[i][h] * a[h:h + 1, :]
    rsum_ref[...] = rsum

    @pl.when(s == pl.num_programs(1) - 1)
    def _():
        o_ref[0] = jnp.sum(acc_ref[...], axis=2)


def _sb_decode(q, pool_kt, pool_vt, layer, page_table, bias_rep, tri, scale):
    b, d = q.shape
    n_pages = page_table.shape[1]
    heads, hd, page = pool_kt.shape[2:]
    pages = min(PAGES_PER_STEP, n_pages)
    steps = n_pages // pages

    def page_spec(i):
        return pl.BlockSpec((None, None, heads, hd, page),
                            lambda bb, s, pt: (layer, pt[bb, (steps - 1 - s) * pages + i], 0, 0, 0))

    const = lambda shape: pl.BlockSpec(shape, lambda bb, s, pt: (0,) * len(shape))
    kern = functools.partial(_sb_decode_kernel, pages=pages, scale=scale)
    return pl.pallas_call(
        kern,
        grid_spec=pltpu.PrefetchScalarGridSpec(
            num_scalar_prefetch=1, grid=(b, steps),
            in_specs=[pl.BlockSpec((1, 1, d), lambda bb, s, pt: (bb, 0, 0)),
                      const(bias_rep.shape), const(tri.shape)]
                     + [page_spec(i) for i in range(pages)] * 2,
            out_specs=pl.BlockSpec((1, heads, hd), lambda bb, s, pt: (bb, 0, 0)),
            scratch_shapes=[pltpu.VMEM((heads, hd, page), F32), pltpu.VMEM((heads, page), F32),
                            pltpu.VMEM((heads, hd, page), F32)]),
        out_shape=jax.ShapeDtypeStruct((b, heads, hd), F32),
        compiler_params=_cparams("arbitrary", "arbitrary"), name="sb_decode",
    )(page_table, q.reshape(b, 1, d), bias_rep, tri, *([pool_kt] * pages), *([pool_vt] * pages))


def _ml_chunk_kernel(q_ref, kt_ref, v_ref, og_ref, gt_ref, bg_ref, hg_ref, tril_ref,
                     o_ref, cx_out_ref, m_out_ref, cx_ref, m_ref, *, heads, dv):
    c = pl.program_id(1)
    tl = gt_ref.shape[0]

    @pl.when(c == 0)
    def _():
        cx_ref[...] = jnp.zeros_like(cx_ref)
        m_ref[...] = jnp.zeros_like(m_ref)

    gts = gt_ref[...] + bg_ref[...]
    lf = -_softplus(-gts)
    bcum = _dot_split_rhs(tril_ref[...], lf, 3)
    bcum = pltpu.roll(bcum, shift=128 - heads, axis=1)
    u_t = (gts - bcum).T
    row = lax.broadcasted_iota(jnp.int32, (tl, tl), 0)
    col = lax.broadcasted_iota(jnp.int32, (tl, tl), 1)
    causal = col <= row
    one_col = (lax.broadcasted_iota(jnp.int32, (tl, dv), 1) == 0).astype(F32)

    for h in range(heads):
        bcol = bcum[:, h:h + 1]
        igcol = gts[:, h:h + 1]
        m_prev = m_ref[h, 0:1, 0:1]
        cx = cx_ref[h]
        dmat = jnp.where(causal, bcol + u_t[h:h + 1, :], -jnp.inf)
        inter = bcol + m_prev
        m_t = jnp.maximum(inter, jnp.max(dmat, axis=1, keepdims=True))
        q = q_ref[h]
        w = jnp.exp(dmat - m_t) * _dot(q, kt_ref[h, 0])
        g = jnp.exp(inter - m_t)
        qc = _dot(q, cx.astype(BF16))
        v = v_ref[h]
        num = _dot(w.astype(BF16), v) + g * qc[:, :dv]
        den = jnp.sum(w, axis=1, keepdims=True) + g * qc[:, dv:dv + 1]
        hout = num * (1.0 / jnp.maximum(jnp.abs(den), jnp.exp(-m_t)))
        hn = _rms(hout, hg_ref[h])
        o_ref[:, h * dv:(h + 1) * dv] = (_sigmoid(og_ref[:, h * dv:(h + 1) * dv]) * hn).astype(BF16)

        m_new = m_t[tl - 1:tl, :]
        b_end = bcol[tl - 1:tl, :]
        wk = jnp.exp(b_end - bcol + igcol - m_new)
        g_end = jnp.exp(b_end + m_prev - m_new)
        vx = (jnp.concatenate([v.astype(F32), one_col], axis=1) * wk).astype(BF16)
        cx_new = g_end * cx + _dot(kt_ref[h, 0], vx)
        cx_ref[h] = cx_new
        m_ref[h] = jnp.broadcast_to(m_new, m_ref.shape[1:])

    @pl.when(c == pl.num_programs(1) - 1)
    def _():
        cx_out_ref[0] = cx_ref[...]
        m_out_ref[0] = m_ref[...]


def _ml_chunk(qb, ktb, vb, og, gates, b_gates_row, head_g, tril, batch):
    heads, m, dqk = qb.shape
    dv = vb.shape[2]
    tl = ktb.shape[3]
    nc = m // batch // tl
    ng = gates.shape[1]
    kern = functools.partial(_ml_chunk_kernel, heads=heads, dv=dv)
    return pl.pallas_call(
        kern, grid=(batch, nc),
        in_specs=[pl.BlockSpec((heads, tl, dqk), lambda b, c: (0, b * nc + c, 0)),
                  pl.BlockSpec((heads, 1, dqk, tl), lambda b, c: (0, b * nc + c, 0, 0)),
                  pl.BlockSpec((heads, tl, dv), lambda b, c: (0, b * nc + c, 0)),
                  pl.BlockSpec((tl, heads * dv), lambda b, c: (b * nc + c, 0)),
                  pl.BlockSpec((tl, ng), lambda b, c: (b * nc + c, 0)),
                  pl.BlockSpec((1, ng), lambda b, c: (0, 0)),
                  pl.BlockSpec((heads, 1, dv), lambda b, c: (0, 0, 0)),
                  pl.BlockSpec((tl, tl), lambda b, c: (0, 0))],
        out_specs=[pl.BlockSpec((tl, heads * dv), lambda b, c: (b * nc + c, 0)),
                   pl.BlockSpec((1, heads, dqk, 2 * dv), lambda b, c: (b, 0, 0, 0)),
                   pl.BlockSpec((1, heads, 8, 128), lambda b, c: (b, 0, 0, 0))],
        out_shape=[jax.ShapeDtypeStruct((m, heads * dv), BF16),
                   jax.ShapeDtypeStruct((batch, heads, dqk, 2 * dv), F32),
                   jax.ShapeDtypeStruct((batch, heads, 8, 128), F32)],
        scratch_shapes=[pltpu.VMEM((heads, dqk, 2 * dv), F32), pltpu.VMEM((heads, 8, 128), F32)],
        compiler_params=_cparams("arbitrary", "arbitrary"), name="ml_chunk",
    )(qb, ktb, vb, og, gates, b_gates_row, head_g, tril)


def _ml_step_kernel(pr_ref, c0_ref, n0_ref, m0_ref, bg_ref, hg_ref,
                    o_ref, c_ref, n_ref, m_ref, *, heads, dqk, dv):
    hq, hv = heads * dqk, heads * dv
    gts = pr_ref[0, :, 2 * hq + 2 * hv:] + bg_ref[...]
    ig = gts[:, 0:heads]
    lf = -_softplus(-gts[:, heads:2 * heads])
    inter = lf + m0_ref[0]
    m_t = jnp.maximum(inter, ig)
    m_ref[0] = m_t
    wgt = jnp.exp(ig - m_t)
    g = jnp.exp(inter - m_t)
    floor = jnp.exp(-m_t)
    eye = (lax.broadcasted_iota(jnp.int32, (dqk, dqk), 0)
           == lax.broadcasted_iota(jnp.int32, (dqk, dqk), 1))

    def column(r):
        return jnp.sum(jnp.where(eye, r, 0.0), axis=1, keepdims=True)

    for h in range(heads):
        q = pr_ref[0, :, h * dqk:(h + 1) * dqk] * (dqk ** -0.5)
        k = pr_ref[0, :, hq + h * dqk:hq + (h + 1) * dqk]
        v = pr_ref[0, :, 2 * hq + h * dv:2 * hq + (h + 1) * dv]
        og = pr_ref[0, :, 2 * hq + hv + h * dv:2 * hq + hv + (h + 1) * dv]
        c0 = c0_ref[0, h]
        n0 = n0_ref[0, h:h + 1, :]
        w_h, g_h = wgt[:, h:h + 1], g[:, h:h + 1]
        qk = jnp.sum(q * k, axis=1, keepdims=True)
        num = (w_h * qk) * v + g_h * jnp.sum(column(q) * c0, axis=0, keepdims=True)
        den = w_h * qk + g_h * jnp.sum(q * n0, axis=1, keepdims=True)
        hout = num * (1.0 / jnp.maximum(jnp.abs(den), floor[:, h:h + 1]))
        o_ref[0, :, h * dv:(h + 1) * dv] = _sigmoid(og) * _rms(hout, hg_ref[h])
        c_ref[0, h] = g_h * c0 + (w_h * column(k)) * v
        n_ref[0, h:h + 1, :] = g_h * n0 + w_h * k


def _ml_step(proj, c0, n0, m0, b_gates_row, head_g, heads, dqk, dv):
    b, n = proj.shape
    hv = heads * dv
    kern = functools.partial(_ml_step_kernel, heads=heads, dqk=dqk, dv=dv)
    out, c, nn, mm = pl.pallas_call(
        kern, grid=(b,),
        in_specs=[pl.BlockSpec((1, 1, n), lambda i: (i, 0, 0)),
                  pl.BlockSpec((1, heads, dqk, dv), lambda i: (i, 0, 0, 0)),
                  pl.BlockSpec((1, heads, dqk), lambda i: (i, 0, 0)),
                  pl.BlockSpec((1, 1, heads), lambda i: (i, 0, 0)),
                  pl.BlockSpec((1, b_gates_row.shape[1]), lambda i: (0, 0)),
                  pl.BlockSpec((heads, 1, dv), lambda i: (0, 0, 0))],
        out_specs=[pl.BlockSpec((1, 1, hv), lambda i: (i, 0, 0)),
                   pl.BlockSpec((1, heads, dqk, dv), lambda i: (i, 0, 0, 0)),
                   pl.BlockSpec((1, heads, dqk), lambda i: (i, 0, 0)),
                   pl.BlockSpec((1, 1, heads), lambda i: (i, 0, 0))],
        out_shape=[jax.ShapeDtypeStruct((b, 1, hv), F32),
                   jax.ShapeDtypeStruct((b, heads, dqk, dv), F32),
                   jax.ShapeDtypeStruct((b, heads, dqk), F32),
                   jax.ShapeDtypeStruct((b, 1, heads), F32)],
        compiler_params=_cparams("arbitrary"), name="ml_step",
    )(proj.reshape(b, 1, n), c0, n0, m0.reshape(b, 1, heads), b_gates_row, head_g)
    return out.reshape(b, hv), c, nn, mm.reshape(b, heads)


def _suffix_ones(n):
    i = lax.broadcasted_iota(jnp.int32, (n, n), 0)
    j = lax.broadcasted_iota(jnp.int32, (n, n), 1)
    return (i >= j).astype(BF16)


def kernel(x_prompt, x_sample, cache_k, cache_v, state_C, state_n, state_m, page_table, p_prompt, p_sample, norm_g, ffn_w_in, ffn_w_out, sb_w_qkv, sb_w_o, sb_logit_bias, ml_w_in, ml_b_gates, ml_head_g, ml_w_out, ple_w_proj, ple_w_gate, final_norm_g):
    batch, seq, d = x_prompt.shape
    dec_batch = x_sample.shape[0]
    depth = norm_g.shape[0]
    sb_heads = sb_logit_bias.shape[1]
    hd = d // sb_heads
    ml_heads, dv = ml_head_g.shape[1], ml_head_g.shape[2]
    dqk = (ml_w_in.shape[2] - 2 * ml_heads * dv - 2 * ml_heads) // (2 * ml_heads)
    page = cache_k.shape[2]
    mp = batch * seq

    xp = x_prompt.reshape(mp, d)
    xs = x_sample.reshape(dec_batch, d)
    pp = p_prompt.reshape(depth, mp, -1)
    ps = p_sample.reshape(depth, dec_batch, -1)
    pool_kt = jnp.transpose(cache_k, (0, 1, 3, 4, 2))
    pool_vt = jnp.transpose(cache_v, (0, 1, 3, 4, 2))

    sb_tile = min(SB_TILE, seq)
    ml_tile = min(ML_TILE, seq)
    tri_sb = _suffix_ones(sb_tile)
    tril_ml = _suffix_ones(ml_tile)
    tri_dec = _suffix_ones(page)

    gate_pad = 128 - 2 * ml_heads
    norm_rows = norm_g.reshape(depth, 4, 1, d)
    final_row = final_norm_g.reshape(1, d)

    kp_l, vp_l, ks_l, vs_l = [], [], [], []
    cp_l, np_l, mp_l, cs_l, ns_l, ms_l = [], [], [], [], [], []
    for i in range(depth):
        g = norm_rows[i]
        j = i // 2
        w_in_a, w_in_b = ffn_w_in[i, 0].astype(BF16), ffn_w_in[i, 1].astype(BF16)
        w_out_a, w_out_b = ffn_w_out[i, 0].astype(BF16), ffn_w_out[i, 1].astype(BF16)
        xp = _ffn(xp, g[0], w_in_a, w_out_a)
        xs = _ffn(xs, g[0], w_in_a, w_out_a)
        if i % 2 == 0:
            w_qkv = sb_w_qkv[j].astype(BF16)
            w_o = sb_w_o[j].astype(BF16)
            kf, vf, qb, ktb, vb = _sb_proj(xp, g[1], w_qkv, sb_heads, sb_tile, batch)
            op = _sb_attn(qb, ktb, vb, sb_logit_bias[j], tri_sb, batch)
            kp_l.append(jnp.transpose(kf, (0, 3, 1, 2)))
            vp_l.append(jnp.transpose(vf, (0, 3, 1, 2)))
            qkv_s = _rms_matmul(xs, g[1], w_qkv)
            ks_l.append(qkv_s[:, d:2 * d].reshape(dec_batch, 1, sb_heads, hd))
            vs_l.append(qkv_s[:, 2 * d:].reshape(dec_batch, 1, sb_heads, hd))
            bias_rep = jnp.broadcast_to(sb_logit_bias[j][:, None], (sb_heads, page))
            os_ = _sb_decode(qkv_s[:, :d], pool_kt, pool_vt, j, page_table, bias_rep,
                             tri_dec, hd ** -0.5).reshape(dec_batch, d)
        else:
            w_in = jnp.pad(ml_w_in[j], ((0, 0), (0, gate_pad))).astype(BF16)
            w_o = ml_w_out[j].astype(BF16)
            bg_row = jnp.pad(ml_b_gates[j], (0, gate_pad)).reshape(1, 128)
            hg = ml_head_g[j].reshape(ml_heads, 1, dv)
            og, gates, qb, ktb, vb = _ml_proj(xp, g[1], w_in, ml_heads, dqk, dv, ml_tile)
            op, cx, mm = _ml_chunk(qb, ktb, vb, og, gates, bg_row, hg, tril_ml, batch)
            cp_l.append(cx[..., :dv])
            np_l.append(cx[..., dv])
            mp_l.append(mm[:, :, 0, 0])
            proj_s = _rms_matmul(xs, g[1], w_in)
            os_, c_s, n_s, m_s = _ml_step(proj_s, state_C[j], state_n[j], state_m[j], bg_row, hg,
                                          ml_heads, dqk, dv)
            cs_l.append(c_s)
            ns_l.append(n_s)
            ms_l.append(m_s)
        xp = _out_proj(op, w_o, xp)
        xs = _out_proj(os_, w_o, xs)
        xp = _ffn(xp, g[2], w_in_b, w_out_b)
        xs = _ffn(xs, g[2], w_in_b, w_out_b)
        w_gate, w_proj = ple_w_gate[i].astype(BF16), ple_w_proj[i].astype(BF16)
        g_final = final_row if i == depth - 1 else None
        xp = _ple(xp, pp[i], g[3], w_gate, w_proj, g_final)
        xs = _ple(xs, ps[i], g[3], w_gate, w_proj, g_final)
    return (xp.reshape(batch, seq, d), xs.reshape(dec_batch, 1, d),
            jnp.stack(kp_l), jnp.stack(vp_l), jnp.stack(cp_l), jnp.stack(np_l), jnp.stack(mp_l),
            jnp.stack(ks_l), jnp.stack(vs_l), jnp.stack(cs_l), jnp.stack(ns_l), jnp.stack(ms_l))
```

```python
import functools

import jax
import jax.numpy as jnp
from jax import lax
from jax.experimental import pallas as pl
from jax.experimental.pallas import tpu as pltpu

F32 = jnp.float32
BF16 = jnp.bfloat16

RMS_EPS = 1e-6
LOG2E = 1.4426950408889634
MXU_COLS = 256
VMEM_LIMIT_BYTES = 56 * 1024 * 1024
ROW_TILE = 512
SB_TILE = 256
SB_HEADS_PER_STEP = 4
ML_TILE = 256
PAGES_PER_STEP = 8


def _cparams(*sem):
    return pltpu.CompilerParams(dimension_semantics=sem, vmem_limit_bytes=VMEM_LIMIT_BYTES)


def _const_spec(shape):
    nd = len(shape)
    return pl.BlockSpec(shape, lambda *_: (0,) * nd, pipeline_mode=pl.Buffered(1))


def _dot(a, b):
    return jnp.dot(a, b, preferred_element_type=F32)


def _rms(x, g):
    return x * lax.rsqrt(jnp.mean(x * x, axis=-1, keepdims=True) + RMS_EPS) * g


def _sigmoid(x):
    return 1.0 / (1.0 + jnp.exp(-x))


def _softplus(z):
    return jnp.maximum(z, 0.0) + jnp.log(1.0 + jnp.exp2(jnp.abs(z) * (-LOG2E)))


def _split_bf16(a, parts):
    out = []
    for _ in range(parts - 1):
        hi = a.astype(BF16)
        out.append(hi)
        a = a - hi.astype(F32)
    out.append(a.astype(BF16))
    return out


def _dot_split_lhs(a, b01, parts):
    acc = None
    for p in _split_bf16(a, parts):
        y = _dot(p, b01)
        acc = y if acc is None else acc + y
    return acc


def _dot_split_rhs(a01, b, parts):
    acc = None
    for p in _split_bf16(b, parts):
        y = _dot(a01, p)
        acc = y if acc is None else acc + y
    return acc


def _col_chunks(n, max_cols):
    assert n % MXU_COLS == 0
    step = max(MXU_COLS, (max_cols // MXU_COLS) * MXU_COLS)
    return [(c, min(c + step, n)) for c in range(0, n, step)]


def _ffn_kernel(x_ref, g_ref, win_ref, wout_ref, o_ref, *, d_ff, chunks):
    x = x_ref[...]
    h = _rms(x, g_ref[...]).astype(BF16)
    acc = None
    for c0, c1 in chunks:
        gate = _dot(h, win_ref[:, c0:c1])
        up = _dot(h, win_ref[:, d_ff + c0:d_ff + c1])
        act = (gate * _sigmoid(gate) * up).astype(BF16)
        y = _dot(act, wout_ref[c0:c1, :])
        acc = y if acc is None else acc + y
    o_ref[...] = x + 0.5 * acc


def _ffn(x, g, w_in, w_out):
    m, d = x.shape
    d_ff = w_out.shape[0]
    tm = min(ROW_TILE, m)
    kern = functools.partial(_ffn_kernel, d_ff=d_ff, chunks=_col_chunks(d_ff, 1536))
    return pl.pallas_call(
        kern, grid=(m // tm,),
        in_specs=[pl.BlockSpec((tm, d), lambda i: (i, 0)),
                  _const_spec((1, d)), _const_spec(w_in.shape), _const_spec(w_out.shape)],
        out_specs=pl.BlockSpec((tm, d), lambda i: (i, 0)),
        out_shape=jax.ShapeDtypeStruct((m, d), F32),
        compiler_params=_cparams("arbitrary"), name="ffn",
    )(x, g, w_in, w_out)


def _rms_matmul_kernel(x_ref, g_ref, w_ref, o_ref):
    h = _rms(x_ref[...], g_ref[...]).astype(BF16)
    o_ref[...] = _dot(h, w_ref[...])


def _rms_matmul(x, g, w):
    m, d = x.shape
    n = w.shape[1]
    return pl.pallas_call(
        _rms_matmul_kernel, grid=(1,),
        in_specs=[_const_spec((m, d)), _const_spec((1, d)), _const_spec(w.shape)],
        out_specs=pl.BlockSpec((m, n), lambda i: (0, 0)),
        out_shape=jax.ShapeDtypeStruct((m, n), F32),
        compiler_params=_cparams("arbitrary"), name="rms_matmul",
    )(x, g, w)


def _head_major_stores(q, kt, v, qb_ref, ktb_ref, vb_ref, *, heads, dqk, dv, tk):
    tm = q.shape[0]
    for h in range(heads):
        qb_ref[h] = q[:, h * dqk:(h + 1) * dqk].astype(BF16)
        vb_ref[h] = v[:, h * dv:(h + 1) * dv].astype(BF16)
        for r in range(tm // tk):
            ktb_ref[h, r] = kt[h * dqk:(h + 1) * dqk, r * tk:(r + 1) * tk].astype(BF16)


def _sb_proj_kernel(x_ref, g_ref, w_ref, kf_ref, vf_ref, qb_ref, ktb_ref, vb_ref, *, heads, hd, tk):
    d = heads * hd
    h = _rms(x_ref[...], g_ref[...]).astype(BF16)
    q = _dot(h, w_ref[:, 0:d]) * (hd ** -0.5)
    k = _dot(h, w_ref[:, d:2 * d])
    v = _dot(h, w_ref[:, 2 * d:3 * d])
    kt = k.T
    kf_ref[0] = kt.reshape(heads, hd, kt.shape[1])
    vf_ref[0] = v.T.reshape(heads, hd, kt.shape[1])
    _head_major_stores(q, kt, v, qb_ref, ktb_ref, vb_ref, heads=heads, dqk=hd, dv=hd, tk=tk)


def _sb_proj(x, g, w, heads, tk, batch):
    m, d = x.shape
    hd = d // heads
    seq = m // batch
    tm = min(ROW_TILE, seq)
    spb = seq // tm
    kv_spec = pl.BlockSpec((1, heads, hd, tm), lambda i: (i // spb, 0, 0, i % spb))
    kern = functools.partial(_sb_proj_kernel, heads=heads, hd=hd, tk=tk)
    return pl.pallas_call(
        kern, grid=(m // tm,),
        in_specs=[pl.BlockSpec((tm, d), lambda i: (i, 0)), _const_spec((1, d)), _const_spec(w.shape)],
        out_specs=[kv_spec, kv_spec,
                   pl.BlockSpec((heads, tm, hd), lambda i: (0, i, 0)),
                   pl.BlockSpec((heads, tm // tk, hd, tk), lambda i: (0, i, 0, 0)),
                   pl.BlockSpec((heads, tm, hd), lambda i: (0, i, 0))],
        out_shape=[jax.ShapeDtypeStruct((batch, heads, hd, seq), F32),
                   jax.ShapeDtypeStruct((batch, heads, hd, seq), F32),
                   jax.ShapeDtypeStruct((heads, m, hd), BF16),
                   jax.ShapeDtypeStruct((heads, m // tk, hd, tk), BF16),
                   jax.ShapeDtypeStruct((heads, m, hd), BF16)],
        compiler_params=_cparams("arbitrary"), name="sb_proj",
    )(x, g, w)


def _ml_proj_kernel(x_ref, g_ref, w_ref, og_ref, gt_ref, qb_ref, ktb_ref, vb_ref,
                    *, heads, dqk, dv, tk):
    hq, hv = heads * dqk, heads * dv
    h = _rms(x_ref[...], g_ref[...]).astype(BF16)
    q = _dot(h, w_ref[:, 0:hq]) * (dqk ** -0.5)
    k = _dot(h, w_ref[:, hq:2 * hq])
    v = _dot(h, w_ref[:, 2 * hq:2 * hq + hv])
    og_ref[...] = _dot(h, w_ref[:, 2 * hq + hv:2 * hq + 2 * hv])
    gt_ref[...] = _dot(h, w_ref[:, 2 * hq + 2 * hv:])
    _head_major_stores(q, k.T, v, qb_ref, ktb_ref, vb_ref, heads=heads, dqk=dqk, dv=dv, tk=tk)


def _ml_proj(x, g, w, heads, dqk, dv, tk):
    m, d = x.shape
    hv = heads * dv
    ng = w.shape[1] - 2 * heads * dqk - 2 * hv
    tm = min(ROW_TILE, m)
    kern = functools.partial(_ml_proj_kernel, heads=heads, dqk=dqk, dv=dv, tk=tk)
    return pl.pallas_call(
        kern, grid=(m // tm,),
        in_specs=[pl.BlockSpec((tm, d), lambda i: (i, 0)), _const_spec((1, d)), _const_spec(w.shape)],
        out_specs=[pl.BlockSpec((tm, hv), lambda i: (i, 0)),
                   pl.BlockSpec((tm, ng), lambda i: (i, 0)),
                   pl.BlockSpec((heads, tm, dqk), lambda i: (0, i, 0)),
                   pl.BlockSpec((heads, tm // tk, dqk, tk), lambda i: (0, i, 0, 0)),
                   pl.BlockSpec((heads, tm, dv), lambda i: (0, i, 0))],
        out_shape=[jax.ShapeDtypeStruct((m, hv), F32), jax.ShapeDtypeStruct((m, ng), F32),
                   jax.ShapeDtypeStruct((heads, m, dqk), BF16),
                   jax.ShapeDtypeStruct((heads, m // tk, dqk, tk), BF16),
                   jax.ShapeDtypeStruct((heads, m, dv), BF16)],
        compiler_params=_cparams("arbitrary"), name="ml_proj",
    )(x, g, w)


def _out_proj_kernel(a_ref, w_ref, x_ref, o_ref):
    o_ref[...] = x_ref[...] + _dot(a_ref[...].astype(BF16), w_ref[...])


def _out_proj(a, w, x):
    m, d = x.shape
    k = a.shape[1]
    tm = min(ROW_TILE, m)
    return pl.pallas_call(
        _out_proj_kernel, grid=(m // tm,),
        in_specs=[pl.BlockSpec((tm, k), lambda i: (i, 0)), _const_spec(w.shape),
                  pl.BlockSpec((tm, d), lambda i: (i, 0))],
        out_specs=pl.BlockSpec((tm, d), lambda i: (i, 0)),
        out_shape=jax.ShapeDtypeStruct((m, d), F32),
        compiler_params=_cparams("arbitrary"), name="out_proj",
    )(a, w, x)


def _ple_kernel(x_ref, p_ref, g_ref, wg_ref, wp_ref, *refs, final):
    x = x_ref[...]
    gate = _sigmoid(_dot(_rms(x, g_ref[...]).astype(BF16), wg_ref[...]))
    out = x + gate * _dot(p_ref[...].astype(BF16), wp_ref[...])
    if final:
        gf_ref, y_ref = refs
        y_ref[...] = _rms(out, gf_ref[...])
    else:
        refs[0][...] = out


def _ple(x, p, g, w_gate, w_proj, g_final=None):
    m, d = x.shape
    dp = p.shape[1]
    tm = min(ROW_TILE, m)
    row = pl.BlockSpec((tm, d), lambda i: (i, 0))
    final = g_final is not None
    extra_specs, extra_args = ([_const_spec((1, d))], [g_final]) if final else ([], [])
    return pl.pallas_call(
        functools.partial(_ple_kernel, final=final), grid=(m // tm,),
        in_specs=[row, pl.BlockSpec((tm, dp), lambda i: (i, 0)), _const_spec((1, d)),
                  _const_spec(w_gate.shape), _const_spec(w_proj.shape)] + extra_specs,
        out_specs=row,
        out_shape=jax.ShapeDtypeStruct((m, d), F32),
        compiler_params=_cparams("arbitrary"), name="ple",
    )(x, p, g, w_gate, w_proj, *extra_args)


MASKED_LOGIT = -1e30


def _sb_attn_kernel(bias_ref, q_ref, kt_ref, v_ref, tri_ref, o_ref,
                    rsum_ref, acc_ref, z_ref, hi_ref, lo_ref, *, tq, heads_per_step):
    hp = pl.program_id(1)
    i = pl.program_id(2)
    tri = tri_ref[...]

    def scores(t, slot, masked=False):
        for hh in range(heads_per_step):
            z = _dot(q_ref[hh], kt_ref[hh, i - t]) + bias_ref[hp * heads_per_step + hh]
            sp = _softplus(z)
            if masked:
                row = lax.broadcasted_iota(jnp.int32, (tq, tq), 0)
                col = lax.broadcasted_iota(jnp.int32, (tq, tq), 1)
                sp = jnp.where(col < row, sp, 0.0)
                z = jnp.where(col < row, z, MASKED_LOGIT)
            hi, lo = _split_bf16(sp, 2)
            z_ref[slot, hh] = z
            hi_ref[slot, hh] = hi
            lo_ref[slot, hh] = lo

    def weights(t, slot):
        start = pl.multiple_of((i - t) * tq, tq)
        for hh in range(heads_per_step):
            inc_hi = _dot(hi_ref[slot, hh], tri)
            inc_lo = _dot(lo_ref[slot, hh], tri)
            rsum = rsum_ref[hh]
            base = z_ref[slot, hh] - jnp.concatenate([rsum] * (tq // 128), axis=1)
            a = jnp.exp((base - inc_hi) - inc_lo)
            acc_ref[hh] += _dot(a.astype(BF16), v_ref[hh, pl.ds(start, tq), :])
            rsum_ref[hh] = rsum + jnp.broadcast_to(inc_hi[:, 0:1] + inc_lo[:, 0:1], rsum.shape)

    rsum_ref[...] = jnp.zeros_like(rsum_ref)
    acc_ref[...] = jnp.zeros_like(acc_ref)
    scores(0, 0, masked=True)

    def pair(p, carry):
        t = 2 * p + 1
        weights(t - 1, 0)
        scores(t, 1)
        weights(t, 1)
        scores(t + 1, 0)
        return carry

    lax.fori_loop(0, i // 2, pair, 0)

    @pl.when(i % 2 == 1)
    def _():
        scores(i, 1)
        weights(i - 1, 0)
        weights(i, 1)

    @pl.when(i % 2 == 0)
    def _():
        weights(i, 0)

    o_ref[...] = jnp.concatenate([acc_ref[hh] for hh in range(heads_per_step)], axis=1).astype(BF16)


def _sb_attn(qb, ktb, vb, bias, tri, batch):
    heads, m, hd = qb.shape
    seq = m // batch
    tq = ktb.shape[3]
    nq = seq // tq
    hps = SB_HEADS_PER_STEP
    kern = functools.partial(_sb_attn_kernel, tq=tq, heads_per_step=hps)
    return pl.pallas_call(
        kern, grid=(batch, heads // hps, nq),
        in_specs=[pl.BlockSpec(memory_space=pltpu.SMEM),
                  pl.BlockSpec((hps, tq, hd), lambda b, h, i: (h, b * nq + i, 0)),
                  pl.BlockSpec((hps, nq, hd, tq), lambda b, h, i: (h, b, 0, 0)),
                  pl.BlockSpec((hps, seq, hd), lambda b, h, i: (h, b, 0)),
                  _const_spec(tri.shape)],
        out_specs=pl.BlockSpec((tq, hps * hd), lambda b, h, i: (b * nq + i, h)),
        out_shape=jax.ShapeDtypeStruct((m, heads * hd), BF16),
        scratch_shapes=[pltpu.VMEM((hps, tq, 128), F32), pltpu.VMEM((hps, tq, hd), F32),
                        pltpu.VMEM((2, hps, tq, tq), F32), pltpu.VMEM((2, hps, tq, tq), BF16),
                        pltpu.VMEM((2, hps, tq, tq), BF16)],
        compiler_params=_cparams("arbitrary", "arbitrary", "arbitrary"), name="sb_attn",
    )(bias, qb, ktb, vb, tri)


def _sb_decode_kernel(pt_ref, q_ref, bias_ref, tri_ref, *refs, pages, scale):
    del pt_ref
    k_refs, v_refs = refs[:pages], refs[pages:2 * pages]
    o_ref, qrep_ref, rsum_ref, acc_ref = refs[2 * pages:]
    s = pl.program_id(1)
    heads, hd, page = k_refs[0].shape

    @pl.when(s == 0)
    def _():
        qrow = q_ref[0] * scale
        qrep_ref[...] = jnp.broadcast_to(qrow, (page, heads * hd)).T.reshape(heads, hd, page)
        rsum_ref[...] = jnp.zeros_like(rsum_ref)
        acc_ref[...] = jnp.zeros_like(acc_ref)

    qrep = qrep_ref[...]
    bias = bias_ref[...]
    z = jnp.concatenate([jnp.sum(k_refs[i][...] * qrep, axis=1) + bias for i in range(pages)], axis=0)
    sp = _softplus(z)
    inc = _dot_split_lhs(sp, tri_ref[...], 2)
    rsum = rsum_ref[...]
    for i in reversed(range(pages)):
        rows = slice(i * heads, (i + 1) * heads)
        a = jnp.exp(z[rows] - inc[rows] - rsum)
        rsum = rsum + jnp.broadcast_to(inc[rows, 0:1], rsum.shape)
        for h in range(heads):
            acc_ref[h] += v_refs[i][h] * a[h:h + 1, :]
    rsum_ref[...] = rsum

    @pl.when(s == pl.num_programs(1) - 1)
    def _():
        o_ref[0] = jnp.sum(acc_ref[...], axis=2)


def _sb_decode(q, pool_kt, pool_vt, layer, page_table, bias_rep, tri, scale):
    b, d = q.shape
    n_pages = page_table.shape[1]
    heads, hd, page = pool_kt.shape[2:]
    pages = min(PAGES_PER_STEP, n_pages)
    steps = n_pages // pages

    def page_spec(i):
        return pl.BlockSpec((None, None, heads, hd, page),
                            lambda bb, s, pt: (layer, pt[bb, (steps - 1 - s) * pages + i], 0, 0, 0))

    const = lambda shape: pl.BlockSpec(shape, lambda bb, s, pt: (0,) * len(shape))
    kern = functools.partial(_sb_decode_kernel, pages=pages, scale=scale)
    return pl.pallas_call(
        kern,
        grid_spec=pltpu.PrefetchScalarGridSpec(
            num_scalar_prefetch=1, grid=(b, steps),
            in_specs=[pl.BlockSpec((1, 1, d), lambda bb, s, pt: (bb, 0, 0)),
                      const(bias_rep.shape), const(tri.shape)]
                     + [page_spec(i) for i in range(pages)] * 2,
            out_specs=pl.BlockSpec((1, heads, hd), lambda bb, s, pt: (bb, 0, 0)),
            scratch_shapes=[pltpu.VMEM((heads, hd, page), F32), pltpu.VMEM((heads, page), F32),
                            pltpu.VMEM((heads, hd, page), F32)]),
        out_shape=jax.ShapeDtypeStruct((b, heads, hd), F32),
        compiler_params=_cparams("arbitrary", "arbitrary"), name="sb_decode",
    )(page_table, q.reshape(b, 1, d), bias_rep, tri, *([pool_kt] * pages), *([pool_vt] * pages))


def _ml_chunk_kernel(q_ref, kt_ref, v_ref, og_ref, gt_ref, bg_ref, hg_ref, tril_ref,
                     o_ref, cx_out_ref, m_out_ref, cx_ref, m_ref, *, heads, dv):
    c = pl.program_id(1)
    tl = gt_ref.shape[0]

    @pl.when(c == 0)
    def _():
        cx_ref[...] = jnp.zeros_like(cx_ref)
        m_ref[...] = jnp.zeros_like(m_ref)

    gts = gt_ref[...] + bg_ref[...]
    lf = -_softplus(-gts)
    bcum = _dot_split_rhs(tril_ref[...], lf, 3)
    bcum = pltpu.roll(bcum, shift=128 - heads, axis=1)
    u_t = (gts - bcum).T
    row = lax.broadcasted_iota(jnp.int32, (tl, tl), 0)
    col = lax.broadcasted_iota(jnp.int32, (tl, tl), 1)
    causal = col <= row
    one_col = (lax.broadcasted_iota(jnp.int32, (tl, dv), 1) == 0).astype(F32)

    for h in range(heads):
        bcol = bcum[:, h:h + 1]
        igcol = gts[:, h:h + 1]
        m_prev = m_ref[h, 0:1, 0:1]
        cx = cx_ref[h]
        dmat = jnp.where(causal, bcol + u_t[h:h + 1, :], -jnp.inf)
        inter = bcol + m_prev
        m_t = jnp.maximum(inter, jnp.max(dmat, axis=1, keepdims=True))
        q = q_ref[h]
        w = jnp.exp(dmat - m_t) * _dot(q, kt_ref[h, 0])
        g = jnp.exp(inter - m_t)
        qc = _dot(q, cx.astype(BF16))
        v = v_ref[h]
        num = _dot(w.astype(BF16), v) + g * qc[:, :dv]
        den = jnp.sum(w, axis=1, keepdims=True) + g * qc[:, dv:dv + 1]
        hout = num * (1.0 / jnp.maximum(jnp.abs(den), jnp.exp(-m_t)))
        hn = _rms(hout, hg_ref[h])
        o_ref[:, h * dv:(h + 1) * dv] = (_sigmoid(og_ref[:, h * dv:(h + 1) * dv]) * hn).astype(BF16)

        m_new = m_t[tl - 1:tl, :]
        b_end = bcol[tl - 1:tl, :]
        wk = jnp.exp(b_end - bcol + igcol - m_new)
        g_end = jnp.exp(b_end + m_prev - m_new)
        vx = (jnp.concatenate([v.astype(F32), one_col], axis=1) * wk).astype(BF16)
        cx_new = g_end * cx + _dot(kt_ref[h, 0], vx)
        cx_ref[h] = cx_new
        m_ref[h] = jnp.broadcast_to(m_new, m_ref.shape[1:])

    @pl.when(c == pl.num_programs(1) - 1)
    def _():
        cx_out_ref[0] = cx_ref[...]
        m_out_ref[0] = m_ref[...]


def _ml_chunk(qb, ktb, vb, og, gates, b_gates_row, head_g, tril, batch):
    heads, m, dqk = qb.shape
    dv = vb.shape[2]
    tl = ktb.shape[3]
    nc = m // batch // tl
    ng = gates.shape[1]
    kern = functools.partial(_ml_chunk_kernel, heads=heads, dv=dv)
    return pl.pallas_call(
        kern, grid=(batch, nc),
        in_specs=[pl.BlockSpec((heads, tl, dqk), lambda b, c: (0, b * nc + c, 0)),
                  pl.BlockSpec((heads, 1, dqk, tl), lambda b, c: (0, b * nc + c, 0, 0)),
                  pl.BlockSpec((heads, tl, dv), lambda b, c: (0, b * nc + c, 0)),
                  pl.BlockSpec((tl, heads * dv), lambda b, c: (b * nc + c, 0)),
                  pl.BlockSpec((tl, ng), lambda b, c: (b * nc + c, 0)),
                  pl.BlockSpec((1, ng), lambda b, c: (0, 0)),
                  pl.BlockSpec((heads, 1, dv), lambda b, c: (0, 0, 0)),
                  pl.BlockSpec((tl, tl), lambda b, c: (0, 0))],
        out_specs=[pl.BlockSpec((tl, heads * dv), lambda b, c: (b * nc + c, 0)),
                   pl.BlockSpec((1, heads, dqk, 2 * dv), lambda b, c: (b, 0, 0, 0)),
                   pl.BlockSpec((1, heads, 8, 128), lambda b, c: (b, 0, 0, 0))],
        out_shape=[jax.ShapeDtypeStruct((m, heads * dv), BF16),
                   jax.ShapeDtypeStruct((batch, heads, dqk, 2 * dv), F32),
                   jax.ShapeDtypeStruct((batch, heads, 8, 128), F32)],
        scratch_shapes=[pltpu.VMEM((heads, dqk, 2 * dv), F32), pltpu.VMEM((heads, 8, 128), F32)],
        compiler_params=_cparams("arbitrary", "arbitrary"), name="ml_chunk",
    )(qb, ktb, vb, og, gates, b_gates_row, head_g, tril)


def _ml_step_kernel(pr_ref, c0_ref, n0_ref, m0_ref, bg_ref, hg_ref,
                    o_ref, c_ref, n_ref, m_ref, *, heads, dqk, dv):
    hq, hv = heads * dqk, heads * dv
    gts = pr_ref[0, :, 2 * hq + 2 * hv:] + bg_ref[...]
    ig = gts[:, 0:heads]
    lf = -_softplus(-gts[:, heads:2 * heads])
    inter = lf + m0_ref[0]
    m_t = jnp.maximum(inter, ig)
    m_ref[0] = m_t
    wgt = jnp.exp(ig - m_t)
    g = jnp.exp(inter - m_t)
    floor = jnp.exp(-m_t)
    eye = (lax.broadcasted_iota(jnp.int32, (dqk, dqk), 0)
           == lax.broadcasted_iota(jnp.int32, (dqk, dqk), 1))

    hs = range(heads)
    q = [pr_ref[0, :, h * dqk:(h + 1) * dqk] * (dqk ** -0.5) for h in hs]
    k = [pr_ref[0, :, hq + h * dqk:hq + (h + 1) * dqk] for h in hs]
    v = [pr_ref[0, :, 2 * hq + h * dv:2 * hq + (h + 1) * dv] for h in hs]
    n0 = [n0_ref[0, h:h + 1, :] for h in hs]
    qk = [jnp.sum(q[h] * k[h], axis=1, keepdims=True) for h in hs]
    qn = [jnp.sum(q[h] * n0[h], axis=1, keepdims=True) for h in hs]
    qc = [_dot(q[h].astype(BF16), c0_ref[0, h].astype(BF16)) for h in hs]
    kv = [_dot(jnp.where(eye, wgt[:, h:h + 1] * k[h], 0.0).astype(BF16),
               jnp.broadcast_to(v[h], (dqk, dv)).astype(BF16)) for h in hs]
    hout = []
    for h in hs:
        w_h, g_h = wgt[:, h:h + 1], g[:, h:h + 1]
        num = (w_h * qk[h]) * v[h] + g_h * qc[h]
        den = w_h * qk[h] + g_h * qn[h]
        hout.append(num * (1.0 / jnp.maximum(jnp.abs(den), floor[:, h:h + 1])))
        c_ref[0, h] = g_h * c0_ref[0, h] + kv[h]
        n_ref[0, h:h + 1, :] = g_h * n0[h] + w_h * k[h]
    ms = [jnp.mean(hout[h] * hout[h], axis=1, keepdims=True) for h in hs]
    for h in hs:
        og = pr_ref[0, :, 2 * hq + hv + h * dv:2 * hq + hv + (h + 1) * dv]
        hn = hout[h] * lax.rsqrt(ms[h] + RMS_EPS) * hg_ref[h]
        o_ref[0, :, h * dv:(h + 1) * dv] = _sigmoid(og) * hn


def _ml_step(proj, c0, n0, m0, b_gates_row, head_g, heads, dqk, dv):
    b, n = proj.shape
    hv = heads * dv
    kern = functools.partial(_ml_step_kernel, heads=heads, dqk=dqk, dv=dv)
    out, c, nn, mm = pl.pallas_call(
        kern, grid=(b,),
        in_specs=[pl.BlockSpec((1, 1, n), lambda i: (i, 0, 0)),
                  pl.BlockSpec((1, heads, dqk, dv), lambda i: (i, 0, 0, 0)),
                  pl.BlockSpec((1, heads, dqk), lambda i: (i, 0, 0)),
                  pl.BlockSpec((1, 1, heads), lambda i: (i, 0, 0)),
                  pl.BlockSpec((1, b_gates_row.shape[1]), lambda i: (0, 0)),
                  pl.BlockSpec((heads, 1, dv), lambda i: (0, 0, 0))],
        out_specs=[pl.BlockSpec((1, 1, hv), lambda i: (i, 0, 0)),
                   pl.BlockSpec((1, heads, dqk, dv), lambda i: (i, 0, 0, 0)),
                   pl.BlockSpec((1, heads, dqk), lambda i: (i, 0, 0)),
                   pl.BlockSpec((1, 1, heads), lambda i: (i, 0, 0))],
        out_shape=[jax.ShapeDtypeStruct((b, 1, hv), F32),
                   jax.ShapeDtypeStruct((b, heads, dqk, dv), F32),
                   jax.ShapeDtypeStruct((b, heads, dqk), F32),
                   jax.ShapeDtypeStruct((b, 1, heads), F32)],
        compiler_params=_cparams("arbitrary"), name="ml_step",
    )(proj.reshape(b, 1, n), c0, n0, m0.reshape(b, 1, heads), b_gates_row, head_g)
    return out.reshape(b, hv), c, nn, mm.reshape(b, heads)


def _suffix_ones(n):
    i = lax.broadcasted_iota(jnp.int32, (n, n), 0)
    j = lax.broadcasted_iota(jnp.int32, (n, n), 1)
    return (i >= j).astype(BF16)


def kernel(x_prompt, x_sample, cache_k, cache_v, state_C, state_n, state_m, page_table, p_prompt, p_sample, norm_g, ffn_w_in, ffn_w_out, sb_w_qkv, sb_w_o, sb_logit_bias, ml_w_in, ml_b_gates, ml_head_g, ml_w_out, ple_w_proj, ple_w_gate, final_norm_g):
    batch, seq, d = x_prompt.shape
    dec_batch = x_sample.shape[0]
    depth = norm_g.shape[0]
    sb_heads = sb_logit_bias.shape[1]
    hd = d // sb_heads
    ml_heads, dv = ml_head_g.shape[1], ml_head_g.shape[2]
    dqk = (ml_w_in.shape[2] - 2 * ml_heads * dv - 2 * ml_heads) // (2 * ml_heads)
    page = cache_k.shape[2]
    mp = batch * seq

    xp = x_prompt.reshape(mp, d)
    xs = x_sample.reshape(dec_batch, d)
    pp = p_prompt.reshape(depth, mp, -1)
    ps = p_sample.reshape(depth, dec_batch, -1)
    pool_kt = jnp.transpose(cache_k, (0, 1, 3, 4, 2))
    pool_vt = jnp.transpose(cache_v, (0, 1, 3, 4, 2))

    sb_tile = min(SB_TILE, seq)
    ml_tile = min(ML_TILE, seq)
    tri_sb = _suffix_ones(sb_tile)
    tril_ml = _suffix_ones(ml_tile)
    tri_dec = _suffix_ones(page)

    gate_pad = 128 - 2 * ml_heads
    norm_rows = norm_g.reshape(depth, 4, 1, d)
    final_row = final_norm_g.reshape(1, d)

    kp_l, vp_l, ks_l, vs_l = [], [], [], []
    cp_l, np_l, mp_l, cs_l, ns_l, ms_l = [], [], [], [], [], []
    for i in range(depth):
        g = norm_rows[i]
        j = i // 2
        w_in_a, w_in_b = ffn_w_in[i, 0].astype(BF16), ffn_w_in[i, 1].astype(BF16)
        w_out_a, w_out_b = ffn_w_out[i, 0].astype(BF16), ffn_w_out[i, 1].astype(BF16)
        xp = _ffn(xp, g[0], w_in_a, w_out_a)
        xs = _ffn(xs, g[0], w_in_a, w_out_a)
        if i % 2 == 0:
            w_qkv = sb_w_qkv[j].astype(BF16)
            w_o = sb_w_o[j].astype(BF16)
            kf, vf, qb, ktb, vb = _sb_proj(xp, g[1], w_qkv, sb_heads, sb_tile, batch)
            op = _sb_attn(qb, ktb, vb, sb_logit_bias[j], tri_sb, batch)
            kp_l.append(jnp.transpose(kf, (0, 3, 1, 2)))
            vp_l.append(jnp.transpose(vf, (0, 3, 1, 2)))
            qkv_s = _rms_matmul(xs, g[1], w_qkv)
            ks_l.append(qkv_s[:, d:2 * d].reshape(dec_batch, 1, sb_heads, hd))
            vs_l.append(qkv_s[:, 2 * d:].reshape(dec_batch, 1, sb_heads, hd))
            bias_rep = jnp.broadcast_to(sb_logit_bias[j][:, None], (sb_heads, page))
            os_ = _sb_decode(qkv_s[:, :d], pool_kt, pool_vt, j, page_table, bias_rep,
                             tri_dec, hd ** -0.5).reshape(dec_batch, d)
        else:
            w_in = jnp.pad(ml_w_in[j], ((0, 0), (0, gate_pad))).astype(BF16)
            w_o = ml_w_out[j].astype(BF16)
            bg_row = jnp.pad(ml_b_gates[j], (0, gate_pad)).reshape(1, 128)
            hg = ml_head_g[j].reshape(ml_heads, 1, dv)
            og, gates, qb, ktb, vb = _ml_proj(xp, g[1], w_in, ml_heads, dqk, dv, ml_tile)
            op, cx, mm = _ml_chunk(qb, ktb, vb, og, gates, bg_row, hg, tril_ml, batch)
            cp_l.append(cx[..., :dv])
            np_l.append(cx[..., dv])
            mp_l.append(mm[:, :, 0, 0])
            proj_s = _rms_matmul(xs, g[1], w_in)
            os_, c_s, n_s, m_s = _ml_step(proj_s, state_C[j], state_n[j], state_m[j], bg_row, hg,
                                          ml_heads, dqk, dv)
            cs_l.append(c_s)
            ns_l.append(n_s)
            ms_l.append(m_s)
        xp = _out_proj(op, w_o, xp)
        xs = _out_proj(os_, w_o, xs)
        xp = _ffn(xp, g[2], w_in_b, w_out_b)
        xs = _ffn(xs, g[2], w_in_b, w_out_b)
        w_gate, w_proj = ple_w_gate[i].astype(BF16), ple_w_proj[i].astype(BF16)
        g_final = final_row if i == depth - 1 else None
        xp = _ple(xp, pp[i], g[3], w_gate, w_proj, g_final)
        xs = _ple(xs, ps[i], g[3], w_gate, w_proj, g_final)
    return (xp.reshape(batch, seq, d), xs.reshape(dec_batch, 1, d),
            jnp.stack(kp_l), jnp.stack(vp_l), jnp.stack(cp_l), jnp.stack(np_l), jnp.stack(mp_l),
            jnp.stack(ks_l), jnp.stack(vs_l), jnp.stack(cs_l), jnp.stack(ns_l), jnp.stack(ms_l))
```

```python
import functools

import jax
import jax.numpy as jnp
from jax import lax
from jax.experimental import pallas as pl
from jax.experimental.pallas import tpu as pltpu

F32 = jnp.float32
BF16 = jnp.bfloat16

RMS_EPS = 1e-6
LOG2E = 1.4426950408889634
MXU_COLS = 256
VMEM_LIMIT_BYTES = 56 * 1024 * 1024
ROW_TILE = 512
SB_TILE = 256
SB_HEADS_PER_STEP = 4
ML_TILE = 256
PAGES_PER_STEP = 8


def _cparams(*sem):
    return pltpu.CompilerParams(dimension_semantics=sem, vmem_limit_bytes=VMEM_LIMIT_BYTES)


def _const_spec(shape):
    nd = len(shape)
    return pl.BlockSpec(shape, lambda *_: (0,) * nd, pipeline_mode=pl.Buffered(1))


def _dot(a, b):
    return jnp.dot(a, b, preferred_element_type=F32)


def _rms(x, g):
    return x * lax.rsqrt(jnp.mean(x * x, axis=-1, keepdims=True) + RMS_EPS) * g


def _sigmoid(x):
    return 1.0 / (1.0 + jnp.exp(-x))


def _softplus(z):
    return jnp.maximum(z, 0.0) + jnp.log(1.0 + jnp.exp2(jnp.abs(z) * (-LOG2E)))


def _softplus_log2(z2):
    neg_abs = lax.bitcast_convert_type(
        lax.bitcast_convert_type(z2, jnp.int32) | jnp.int32(-2 ** 31), F32)
    return jnp.maximum(z2, 0.0) + jnp.log(1.0 + jnp.exp2(neg_abs)) * LOG2E


def _split_bf16(a, parts):
    out = []
    for _ in range(parts - 1):
        hi = a.astype(BF16)
        out.append(hi)
        a = a - hi.astype(F32)
    out.append(a.astype(BF16))
    return out


def _dot_split_lhs(a, b01, parts):
    acc = None
    for p in _split_bf16(a, parts):
        y = _dot(p, b01)
        acc = y if acc is None else acc + y
    return acc


def _dot_split_rhs(a01, b, parts):
    acc = None
    for p in _split_bf16(b, parts):
        y = _dot(a01, p)
        acc = y if acc is None else acc + y
    return acc


def _col_chunks(n, max_cols):
    assert n % MXU_COLS == 0
    step = max(MXU_COLS, (max_cols // MXU_COLS) * MXU_COLS)
    return [(c, min(c + step, n)) for c in range(0, n, step)]


def _swiglu(h, win_ref, wout_ref, d_ff, chunks):
    acc = None
    for c0, c1 in chunks:
        gate = _dot(h, win_ref[:, c0:c1])
        up = _dot(h, win_ref[:, d_ff + c0:d_ff + c1])
        act = (gate * _sigmoid(gate) * up).astype(BF16)
        y = _dot(act, wout_ref[c0:c1, :])
        acc = y if acc is None else acc + y
    return acc


def _ffn_kernel(x_ref, g_ref, win_ref, wout_ref, o_ref, *, d_ff, chunks):
    x = x_ref[...]
    o_ref[...] = x + 0.5 * _swiglu(_rms(x, g_ref[...]).astype(BF16), win_ref, wout_ref, d_ff, chunks)


def _ffn(x, g, w_in, w_out):
    m, d = x.shape
    d_ff = w_out.shape[0]
    tm = min(ROW_TILE, m)
    kern = functools.partial(_ffn_kernel, d_ff=d_ff, chunks=_col_chunks(d_ff, 1536))
    return pl.pallas_call(
        kern, grid=(m // tm,),
        in_specs=[pl.BlockSpec((tm, d), lambda i: (i, 0)),
                  _const_spec((1, d)), _const_spec(w_in.shape), _const_spec(w_out.shape)],
        out_specs=pl.BlockSpec((tm, d), lambda i: (i, 0)),
        out_shape=jax.ShapeDtypeStruct((m, d), F32),
        compiler_params=_cparams("arbitrary"), name="ffn",
    )(x, g, w_in, w_out)


def _rms_matmul_kernel(x_ref, g_ref, w_ref, o_ref):
    h = _rms(x_ref[...], g_ref[...]).astype(BF16)
    o_ref[...] = _dot(h, w_ref[...])


def _rms_matmul(x, g, w):
    m, d = x.shape
    n = w.shape[1]
    return pl.pallas_call(
        _rms_matmul_kernel, grid=(1,),
        in_specs=[_const_spec((m, d)), _const_spec((1, d)), _const_spec(w.shape)],
        out_specs=pl.BlockSpec((m, n), lambda i: (0, 0)),
        out_shape=jax.ShapeDtypeStruct((m, n), F32),
        compiler_params=_cparams("arbitrary"), name="rms_matmul",
    )(x, g, w)


def _head_major_stores(q, kt, v, qb_ref, ktb_ref, vb_ref, *, heads, dqk, dv, tk):
    tm = q.shape[0]
    for h in range(heads):
        qb_ref[h] = q[:, h * dqk:(h + 1) * dqk].astype(BF16)
        vb_ref[h] = v[:, h * dv:(h + 1) * dv].astype(BF16)
        for r in range(tm // tk):
            ktb_ref[h, r] = kt[h * dqk:(h + 1) * dqk, r * tk:(r + 1) * tk].astype(BF16)


def _sb_proj_kernel(x_ref, g_ref, w_ref, kf_ref, vf_ref, qb_ref, ktb_ref, vb_ref, *, heads, hd, tk):
    d = heads * hd
    h = _rms(x_ref[...], g_ref[...]).astype(BF16)
    q = _dot(h, w_ref[:, 0:d]) * (hd ** -0.5 * LOG2E)
    k = _dot(h, w_ref[:, d:2 * d])
    v = _dot(h, w_ref[:, 2 * d:3 * d])
    kt = k.T
    kf_ref[0] = kt.reshape(heads, hd, kt.shape[1])
    vf_ref[0] = v.T.reshape(heads, hd, kt.shape[1])
    _head_major_stores(q, kt, v, qb_ref, ktb_ref, vb_ref, heads=heads, dqk=hd, dv=hd, tk=tk)


def _sb_proj(x, g, w, heads, tk, batch):
    m, d = x.shape
    hd = d // heads
    seq = m // batch
    tm = min(ROW_TILE, seq)
    spb = seq // tm
    kv_spec = pl.BlockSpec((1, heads, hd, tm), lambda i: (i // spb, 0, 0, i % spb))
    kern = functools.partial(_sb_proj_kernel, heads=heads, hd=hd, tk=tk)
    return pl.pallas_call(
        kern, grid=(m // tm,),
        in_specs=[pl.BlockSpec((tm, d), lambda i: (i, 0)), _const_spec((1, d)), _const_spec(w.shape)],
        out_specs=[kv_spec, kv_spec,
                   pl.BlockSpec((heads, tm, hd), lambda i: (0, i, 0)),
                   pl.BlockSpec((heads, tm // tk, hd, tk), lambda i: (0, i, 0, 0)),
                   pl.BlockSpec((heads, tm, hd), lambda i: (0, i, 0))],
        out_shape=[jax.ShapeDtypeStruct((batch, heads, hd, seq), F32),
                   jax.ShapeDtypeStruct((batch, heads, hd, seq), F32),
                   jax.ShapeDtypeStruct((heads, m, hd), BF16),
                   jax.ShapeDtypeStruct((heads, m // tk, hd, tk), BF16),
                   jax.ShapeDtypeStruct((heads, m, hd), BF16)],
        compiler_params=_cparams("arbitrary"), name="sb_proj",
    )(x, g, w)


def _ml_proj_kernel(x_ref, g_ref, w_ref, og_ref, gt_ref, qb_ref, ktb_ref, vb_ref,
                    *, heads, dqk, dv, tk):
    hq, hv = heads * dqk, heads * dv
    h = _rms(x_ref[...], g_ref[...]).astype(BF16)
    q = _dot(h, w_ref[:, 0:hq]) * (dqk ** -0.5)
    k = _dot(h, w_ref[:, hq:2 * hq])
    v = _dot(h, w_ref[:, 2 * hq:2 * hq + hv])
    og_ref[...] = _dot(h, w_ref[:, 2 * hq + hv:2 * hq + 2 * hv])
    gt_ref[...] = _dot(h, w_ref[:, 2 * hq + 2 * hv:])
    _head_major_stores(q, k.T, v, qb_ref, ktb_ref, vb_ref, heads=heads, dqk=dqk, dv=dv, tk=tk)


def _ml_proj(x, g, w, heads, dqk, dv, tk):
    m, d = x.shape
    hv = heads * dv
    ng = w.shape[1] - 2 * heads * dqk - 2 * hv
    tm = min(ROW_TILE, m)
    kern = functools.partial(_ml_proj_kernel, heads=heads, dqk=dqk, dv=dv, tk=tk)
    return pl.pallas_call(
        kern, grid=(m // tm,),
        in_specs=[pl.BlockSpec((tm, d), lambda i: (i, 0)), _const_spec((1, d)), _const_spec(w.shape)],
        out_specs=[pl.BlockSpec((tm, hv), lambda i: (i, 0)),
                   pl.BlockSpec((tm, ng), lambda i: (i, 0)),
                   pl.BlockSpec((heads, tm, dqk), lambda i: (0, i, 0)),
                   pl.BlockSpec((heads, tm // tk, dqk, tk), lambda i: (0, i, 0, 0)),
                   pl.BlockSpec((heads, tm, dv), lambda i: (0, i, 0))],
        out_shape=[jax.ShapeDtypeStruct((m, hv), F32), jax.ShapeDtypeStruct((m, ng), F32),
                   jax.ShapeDtypeStruct((heads, m, dqk), BF16),
                   jax.ShapeDtypeStruct((heads, m // tk, dqk, tk), BF16),
                   jax.ShapeDtypeStruct((heads, m, dv), BF16)],
        compiler_params=_cparams("arbitrary"), name="ml_proj",
    )(x, g, w)


def _post_mixer_kernel(a_ref, x_ref, p_ref, g2_ref, g3_ref, wo_ref, win_ref, wout_ref, wg_ref, wp_ref,
                       *refs, d_ff, chunks, final):
    x = x_ref[...] + _dot(a_ref[...].astype(BF16), wo_ref[...])
    x = x + 0.5 * _swiglu(_rms(x, g2_ref[...]).astype(BF16), win_ref, wout_ref, d_ff, chunks)
    gate = _sigmoid(_dot(_rms(x, g3_ref[...]).astype(BF16), wg_ref[...]))
    out = x + gate * _dot(p_ref[...].astype(BF16), wp_ref[...])
    if final:
        gf_ref, y_ref = refs
        y_ref[...] = _rms(out, gf_ref[...])
    else:
        refs[0][...] = out


def _post_mixer(a, x, p, g2, g3, w_o, w_in, w_out, w_gate, w_proj, g_final=None):
    m, d = x.shape
    ka, dp, d_ff = a.shape[1], p.shape[1], w_out.shape[0]
    tm = min(ROW_TILE, m)
    row = lambda n: pl.BlockSpec((tm, n), lambda i: (i, 0))
    final = g_final is not None
    extra_specs, extra_args = ([_const_spec((1, d))], [g_final]) if final else ([], [])
    kern = functools.partial(_post_mixer_kernel, d_ff=d_ff, chunks=_col_chunks(d_ff, 1536), final=final)
    return pl.pallas_call(
        kern, grid=(m // tm,),
        in_specs=[row(ka), row(d), row(dp), _const_spec((1, d)), _const_spec((1, d)),
                  _const_spec(w_o.shape), _const_spec(w_in.shape), _const_spec(w_out.shape),
                  _const_spec(w_gate.shape), _const_spec(w_proj.shape)] + extra_specs,
        out_specs=row(d),
        out_shape=jax.ShapeDtypeStruct((m, d), F32),
        compiler_params=_cparams("arbitrary"), name="post_mixer",
    )(a, x, p, g2, g3, w_o, w_in, w_out, w_gate, w_proj, *extra_args)


MASKED_LOGIT = -1e30


def _sb_attn_kernel(bias_ref, q_ref, kt_ref, v_ref, tri_ref, o_ref,
                    rsum_ref, acc_ref, z_ref, hi_ref, lo_ref, *, tq, heads_per_step):
    hp = pl.program_id(1)
    i = pl.program_id(2)
    tri = tri_ref[...]

    all_heads = range(heads_per_step)

    def scores(t, slot, masked=False, heads=all_heads):
        for hh in heads:
            z = _dot(q_ref[hh], kt_ref[hh, i - t]) + bias_ref[hp * heads_per_step + hh] * LOG2E
            sp = _softplus_log2(z)
            if masked:
                row = lax.broadcasted_iota(jnp.int32, (tq, tq), 0)
                col = lax.broadcasted_iota(jnp.int32, (tq, tq), 1)
                sp = jnp.where(col < row, sp, 0.0)
                z = jnp.where(col < row, z, MASKED_LOGIT)
            hi, lo = _split_bf16(sp, 2)
            z_ref[slot, hh] = z
            hi_ref[slot, hh] = hi
            lo_ref[slot, hh] = lo

    def weights(t, slot, heads=all_heads):
        start = pl.multiple_of((i - t) * tq, tq)
        for hh in heads:
            inc_hi = _dot(hi_ref[slot, hh], tri)
            inc_lo = _dot(lo_ref[slot, hh], tri)
            rsum = rsum_ref[hh]
            base = z_ref[slot, hh] - jnp.concatenate([rsum] * (tq // 128), axis=1)
            a = jnp.exp2((base - inc_hi) - inc_lo)
            acc_ref[hh] += _dot(a.astype(BF16), v_ref[hh, pl.ds(start, tq), :])
            rsum_ref[hh] = rsum + jnp.broadcast_to(inc_hi[:, 0:1] + inc_lo[:, 0:1], rsum.shape)

    rsum_ref[...] = jnp.zeros_like(rsum_ref)
    acc_ref[...] = jnp.zeros_like(acc_ref)
    scores(0, 0, masked=True)

    def pair(p, carry):
        t = 2 * p + 1
        for hh in all_heads:
            weights(t - 1, 0, [hh])
            scores(t, 1, heads=[hh])
        for hh in all_heads:
            weights(t, 1, [hh])
            scores(t + 1, 0, heads=[hh])
        return carry

    lax.fori_loop(0, i // 2, pair, 0)

    @pl.when(i % 2 == 1)
    def _():
        for hh in all_heads:
            weights(i - 1, 0, [hh])
            scores(i, 1, heads=[hh])
        weights(i, 1)

    @pl.when(i % 2 == 0)
    def _():
        weights(i, 0)

    o_ref[...] = jnp.concatenate([acc_ref[hh] for hh in range(heads_per_step)], axis=1).astype(BF16)


def _sb_attn(qb, ktb, vb, bias, tri, batch):
    heads, m, hd = qb.shape
    seq = m // batch
    tq = ktb.shape[3]
    nq = seq // tq
    hps = SB_HEADS_PER_STEP
    kern = functools.partial(_sb_attn_kernel, tq=tq, heads_per_step=hps)
    return pl.pallas_call(
        kern, grid=(batch, heads // hps, nq),
        in_specs=[pl.BlockSpec(memory_space=pltpu.SMEM),
                  pl.BlockSpec((hps, tq, hd), lambda b, h, i: (h, b * nq + i, 0)),
                  pl.BlockSpec((hps, nq, hd, tq), lambda b, h, i: (h, b, 0, 0)),
                  pl.BlockSpec((hps, seq, hd), lambda b, h, i: (h, b, 0)),
                  _const_spec(tri.shape)],
        out_specs=pl.BlockSpec((tq, hps * hd), lambda b, h, i: (b * nq + i, h)),
        out_shape=jax.ShapeDtypeStruct((m, heads * hd), BF16),
        scratch_shapes=[pltpu.VMEM((hps, tq, 128), F32), pltpu.VMEM((hps, tq, hd), F32),
                        pltpu.VMEM((2, hps, tq, tq), F32), pltpu.VMEM((2, hps, tq, tq), BF16),
                        pltpu.VMEM((2, hps, tq, tq), BF16)],
        compiler_params=_cparams("arbitrary", "arbitrary", "arbitrary"), name="sb_attn",
    )(bias, qb, ktb, vb, tri)


def _sb_decode_kernel(pt_ref, q_ref, bias_ref, tri_ref, *refs, pages, scale):
    del pt_ref
    k_refs, v_refs = refs[:pages], refs[pages:2 * pages]
    o_ref, qrep_ref, rsum_ref, acc_ref = refs[2 * pages:]
    s = pl.program_id(1)
    heads, hd, page = k_refs[0].shape

    @pl.when(s == 0)
    def _():
        qrow = q_ref[0] * scale
        qrep_ref[...] = jnp.broadcast_to(qrow, (page, heads * hd)).T.reshape(heads, hd, page)
        rsum_ref[...] = jnp.zeros_like(rsum_ref)
        acc_ref[...] = jnp.zeros_like(acc_ref)

    qrep = qrep_ref[...]
    bias = bias_ref[...]
    z = jnp.concatenate([jnp.sum(k_refs[i][...] * qrep, axis=1) + bias for i in range(pages)], axis=0)
    sp = _softplus(z)
    inc = _dot_split_lhs(sp, tri_ref[...], 2)
    rsum = rsum_ref[...]
    for i in reversed(range(pages)):
        rows = slice(i * heads, (i + 1) * heads)
        a = jnp.exp(z[rows] - inc[rows] - rsum)
        rsum = rsum + jnp.broadcast_to(inc[rows, 0:1], rsum.shape)
        for h in range(heads):
            acc_ref[h] += v_refs[i][h] * a[h:h + 1, :]
    rsum_ref[...] = rsum

    @pl.when(s == pl.num_programs(1) - 1)
    def _():
        o_ref[0] = jnp.sum(acc_ref[...], axis=2)


def _sb_decode(q, pool_kt, pool_vt, layer, page_table, bias_rep, tri, scale):
    b, d = q.shape
    n_pages = page_table.shape[1]
    heads, hd, page = pool_kt.shape[2:]
    pages = min(PAGES_PER_STEP, n_pages)
    steps = n_pages // pages

    def page_spec(i):
        return pl.BlockSpec((None, None, heads, hd, page),
                            lambda bb, s, pt: (layer, pt[bb, (steps - 1 - s) * pages + i], 0, 0, 0))

    const = lambda shape: pl.BlockSpec(shape, lambda bb, s, pt: (0,) * len(shape))
    kern = functools.partial(_sb_decode_kernel, pages=pages, scale=scale)
    return pl.pallas_call(
        kern,
        grid_spec=pltpu.PrefetchScalarGridSpec(
            num_scalar_prefetch=1, grid=(b, steps),
            in_specs=[pl.BlockSpec((1, 1, d), lambda bb, s, pt: (bb, 0, 0)),
                      const(bias_rep.shape), const(tri.shape)]
                     + [page_spec(i) for i in range(pages)] * 2,
            out_specs=pl.BlockSpec((1, heads, hd), lambda bb, s, pt: (bb, 0, 0)),
            scratch_shapes=[pltpu.VMEM((heads, hd, page), F32), pltpu.VMEM((heads, page), F32),
                            pltpu.VMEM((heads, hd, page), F32)]),
        out_shape=jax.ShapeDtypeStruct((b, heads, hd), F32),
        compiler_params=_cparams("arbitrary", "arbitrary"), name="sb_decode",
    )(page_table, q.reshape(b, 1, d), bias_rep, tri, *([pool_kt] * pages), *([pool_vt] * pages))


def _ml_chunk_kernel(q_ref, kt_ref, v_ref, og_ref, gt_ref, bg_ref, hg_ref, tril_ref,
                     o_ref, cx_out_ref, m_out_ref, cx_ref, m_ref, *, heads, dv):
    c = pl.program_id(1)
    tl = gt_ref.shape[0]

    @pl.when(c == 0)
    def _():
        cx_ref[...] = jnp.zeros_like(cx_ref)
        m_ref[...] = jnp.zeros_like(m_ref)

    gts = gt_ref[...] + bg_ref[...]
    lf = -_softplus(-gts)
    bcum = _dot_split_rhs(tril_ref[...], lf, 3)
    bcum = pltpu.roll(bcum, shift=128 - heads, axis=1)
    u_t = (gts - bcum).T
    row = lax.broadcasted_iota(jnp.int32, (tl, tl), 0)
    col = lax.broadcasted_iota(jnp.int32, (tl, tl), 1)
    causal = col <= row
    one_col = (lax.broadcasted_iota(jnp.int32, (tl, dv), 1) == 0).astype(F32)

    hs = range(heads)
    bcol = [bcum[:, h:h + 1] for h in hs]
    m_prev = [m_ref[h, 0:1, 0:1] for h in hs]
    cx = [cx_ref[h] for h in hs]
    umat = [jnp.where(causal, u_t[h:h + 1, :], -jnp.inf) for h in hs]
    cm = [jnp.maximum(m_prev[h], jnp.max(umat[h], axis=1, keepdims=True)) for h in hs]
    sqk = [_dot(q_ref[h], kt_ref[h, 0]) for h in hs]
    qc = [_dot(q_ref[h], cx[h].astype(BF16)) for h in hs]
    w = [jnp.exp(umat[h] - cm[h]) * sqk[h] for h in hs]
    g = [jnp.exp(m_prev[h] - cm[h]) for h in hs]
    m_t = [bcol[h] + cm[h] for h in hs]
    num = [_dot(w[h].astype(BF16), v_ref[h]) + g[h] * qc[h][:, :dv] for h in hs]
    den = [jnp.sum(w[h], axis=1, keepdims=True) + g[h] * qc[h][:, dv:dv + 1] for h in hs]
    hout = [num[h] * (1.0 / jnp.maximum(jnp.abs(den[h]), jnp.exp(-m_t[h]))) for h in hs]
    ms = [jnp.mean(hout[h] * hout[h], axis=1, keepdims=True) for h in hs]
    for h in hs:
        hn = hout[h] * lax.rsqrt(ms[h] + RMS_EPS) * hg_ref[h]
        o_ref[:, h * dv:(h + 1) * dv] = (_sigmoid(og_ref[:, h * dv:(h + 1) * dv]) * hn).astype(BF16)
    for h in hs:
        m_new = m_t[h][tl - 1:tl, :]
        b_end = bcol[h][tl - 1:tl, :]
        wk = jnp.exp(b_end - bcol[h] + gts[:, h:h + 1] - m_new)
        g_end = jnp.exp(b_end + m_prev[h] - m_new)
        vx = (jnp.concatenate([v_ref[h].astype(F32), one_col], axis=1) * wk).astype(BF16)
        cx_ref[h] = g_end * cx[h] + _dot(kt_ref[h, 0], vx)
        m_ref[h] = jnp.broadcast_to(m_new, m_ref.shape[1:])

    @pl.when(c == pl.num_programs(1) - 1)
    def _():
        cx_out_ref[0] = cx_ref[...]
        m_out_ref[0] = m_ref[...]


def _ml_chunk(qb, ktb, vb, og, gates, b_gates_row, head_g, tril, batch):
    heads, m, dqk = qb.shape
    dv = vb.shape[2]
    tl = ktb.shape[3]
    nc = m // batch // tl
    ng = gates.shape[1]
    kern = functools.partial(_ml_chunk_kernel, heads=heads, dv=dv)
    return pl.pallas_call(
        kern, grid=(batch, nc),
        in_specs=[pl.BlockSpec((heads, tl, dqk), lambda b, c: (0, b * nc + c, 0)),
                  pl.BlockSpec((heads, 1, dqk, tl), lambda b, c: (0, b * nc + c, 0, 0)),
                  pl.BlockSpec((heads, tl, dv), lambda b, c: (0, b * nc + c, 0)),
                  pl.BlockSpec((tl, heads * dv), lambda b, c: (b * nc + c, 0)),
                  pl.BlockSpec((tl, ng), lambda b, c: (b * nc + c, 0)),
                  pl.BlockSpec((1, ng), lambda b, c: (0, 0)),
                  pl.BlockSpec((heads, 1, dv), lambda b, c: (0, 0, 0)),
                  pl.BlockSpec((tl, tl), lambda b, c: (0, 0))],
        out_specs=[pl.BlockSpec((tl, heads * dv), lambda b, c: (b * nc + c, 0)),
                   pl.BlockSpec((1, heads, dqk, 2 * dv), lambda b, c: (b, 0, 0, 0)),
                   pl.BlockSpec((1, heads, 8, 128), lambda b, c: (b, 0, 0, 0))],
        out_shape=[jax.ShapeDtypeStruct((m, heads * dv), BF16),
                   jax.ShapeDtypeStruct((batch, heads, dqk, 2 * dv), F32),
                   jax.ShapeDtypeStruct((batch, heads, 8, 128), F32)],
        scratch_shapes=[pltpu.VMEM((heads, dqk, 2 * dv), F32), pltpu.VMEM((heads, 8, 128), F32)],
        compiler_params=_cparams("arbitrary", "arbitrary"), name="ml_chunk",
    )(qb, ktb, vb, og, gates, b_gates_row, head_g, tril)


def _ml_step_kernel(pr_ref, c0_ref, n0_ref, m0_ref, bg_ref, hg_ref,
                    o_ref, c_ref, n_ref, m_ref, *, heads, dqk, dv):
    hq, hv = heads * dqk, heads * dv
    gts = pr_ref[0, :, 2 * hq + 2 * hv:] + bg_ref[...]
    ig = gts[:, 0:heads]
    lf = -_softplus(-gts[:, heads:2 * heads])
    inter = lf + m0_ref[0]
    m_t = jnp.maximum(inter, ig)
    m_ref[0] = m_t
    wgt = jnp.exp(ig - m_t)
    g = jnp.exp(inter - m_t)
    floor = jnp.exp(-m_t)
    eye = (lax.broadcasted_iota(jnp.int32, (dqk, dqk), 0)
           == lax.broadcasted_iota(jnp.int32, (dqk, dqk), 1))

    hs = range(heads)
    q = [pr_ref[0, :, h * dqk:(h + 1) * dqk] * (dqk ** -0.5) for h in hs]
    k = [pr_ref[0, :, hq + h * dqk:hq + (h + 1) * dqk] for h in hs]
    v = [pr_ref[0, :, 2 * hq + h * dv:2 * hq + (h + 1) * dv] for h in hs]
    n0 = [n0_ref[0, h:h + 1, :] for h in hs]
    qk = [jnp.sum(q[h] * k[h], axis=1, keepdims=True) for h in hs]
    qn = [jnp.sum(q[h] * n0[h], axis=1, keepdims=True) for h in hs]
    qc = [_dot(q[h].astype(BF16), c0_ref[0, h].astype(BF16)) for h in hs]
    kv = [_dot(jnp.where(eye, wgt[:, h:h + 1] * k[h], 0.0).astype(BF16),
               jnp.broadcast_to(v[h], (dqk, dv)).astype(BF16)) for h in hs]
    hout = []
    for h in hs:
        w_h, g_h = wgt[:, h:h + 1], g[:, h:h + 1]
        num = (w_h * qk[h]) * v[h] + g_h * qc[h]
        den = w_h * qk[h] + g_h * qn[h]
        hout.append(num * (1.0 / jnp.maximum(jnp.abs(den), floor[:, h:h + 1])))
        c_ref[0, h] = g_h * c0_ref[0, h] + kv[h]
        n_ref[0, h:h + 1, :] = g_h * n0[h] + w_h * k[h]
    ms = [jnp.mean(hout[h] * hout[h], axis=1, keepdims=True) for h in hs]
    for h in hs:
        og = pr_ref[0, :, 2 * hq + hv + h * dv:2 * hq + hv + (h + 1) * dv]
        hn = hout[h] * lax.rsqrt(ms[h] + RMS_EPS) * hg_ref[h]
        o_ref[0, :, h * dv:(h + 1) * dv] = _sigmoid(og) * hn


def _ml_step(proj, c0, n0, m0, b_gates_row, head_g, heads, dqk, dv):
    b, n = proj.shape
    hv = heads * dv
    kern = functools.partial(_ml_step_kernel, heads=heads, dqk=dqk, dv=dv)
    out, c, nn, mm = pl.pallas_call(
        kern, grid=(b,),
        in_specs=[pl.BlockSpec((1, 1, n), lambda i: (i, 0, 0)),
                  pl.BlockSpec((1, heads, dqk, dv), lambda i: (i, 0, 0, 0)),
                  pl.BlockSpec((1, heads, dqk), lambda i: (i, 0, 0)),
                  pl.BlockSpec((1, 1, heads), lambda i: (i, 0, 0)),
                  pl.BlockSpec((1, b_gates_row.shape[1]), lambda i: (0, 0)),
                  pl.BlockSpec((heads, 1, dv), lambda i: (0, 0, 0))],
        out_specs=[pl.BlockSpec((1, 1, hv), lambda i: (i, 0, 0)),
                   pl.BlockSpec((1, heads, dqk, dv), lambda i: (i, 0, 0, 0)),
                   pl.BlockSpec((1, heads, dqk), lambda i: (i, 0, 0)),
                   pl.BlockSpec((1, 1, heads), lambda i: (i, 0, 0))],
        out_shape=[jax.ShapeDtypeStruct((b, 1, hv), F32),
                   jax.ShapeDtypeStruct((b, heads, dqk, dv), F32),
                   jax.ShapeDtypeStruct((b, heads, dqk), F32),
                   jax.ShapeDtypeStruct((b, 1, heads), F32)],
        compiler_params=_cparams("arbitrary"), name="ml_step",
    )(proj.reshape(b, 1, n), c0, n0, m0.reshape(b, 1, heads), b_gates_row, head_g)
    return out.reshape(b, hv), c, nn, mm.reshape(b, heads)


def _suffix_ones(n):
    i = lax.broadcasted_iota(jnp.int32, (n, n), 0)
    j = lax.broadcasted_iota(jnp.int32, (n, n), 1)
    return (i >= j).astype(BF16)


def kernel(x_prompt, x_sample, cache_k, cache_v, state_C, state_n, state_m, page_table, p_prompt, p_sample, norm_g, ffn_w_in, ffn_w_out, sb_w_qkv, sb_w_o, sb_logit_bias, ml_w_in, ml_b_gates, ml_head_g, ml_w_out, ple_w_proj, ple_w_gate, final_norm_g):
    batch, seq, d = x_prompt.shape
    dec_batch = x_sample.shape[0]
    depth = norm_g.shape[0]
    sb_heads = sb_logit_bias.shape[1]
    hd = d // sb_heads
    ml_heads, dv = ml_head_g.shape[1], ml_head_g.shape[2]
    dqk = (ml_w_in.shape[2] - 2 * ml_heads * dv - 2 * ml_heads) // (2 * ml_heads)
    page = cache_k.shape[2]
    mp = batch * seq

    xp = x_prompt.reshape(mp, d)
    xs = x_sample.reshape(dec_batch, d)
    pp = p_prompt.reshape(depth, mp, -1)
    ps = p_sample.reshape(depth, dec_batch, -1)
    pool_kt = jnp.transpose(cache_k, (0, 1, 3, 4, 2))
    pool_vt = jnp.transpose(cache_v, (0, 1, 3, 4, 2))

    sb_tile = min(SB_TILE, seq)
    ml_tile = min(ML_TILE, seq)
    tri_sb = _suffix_ones(sb_tile)
    tril_ml = _suffix_ones(ml_tile)
    tri_dec = _suffix_ones(page)

    gate_pad = 128 - 2 * ml_heads
    norm_rows = norm_g.reshape(depth, 4, 1, d)
    final_row = final_norm_g.reshape(1, d)

    kp_l, vp_l, ks_l, vs_l = [], [], [], []
    cp_l, np_l, mp_l, cs_l, ns_l, ms_l = [], [], [], [], [], []
    for i in range(depth):
        g = norm_rows[i]
        j = i // 2
        w_in_a, w_in_b = ffn_w_in[i, 0].astype(BF16), ffn_w_in[i, 1].astype(BF16)
        w_out_a, w_out_b = ffn_w_out[i, 0].astype(BF16), ffn_w_out[i, 1].astype(BF16)
        xp = _ffn(xp, g[0], w_in_a, w_out_a)
        xs = _ffn(xs, g[0], w_in_a, w_out_a)
        if i % 2 == 0:
            w_qkv = sb_w_qkv[j].astype(BF16)
            w_o = sb_w_o[j].astype(BF16)
            kf, vf, qb, ktb, vb = _sb_proj(xp, g[1], w_qkv, sb_heads, sb_tile, batch)
            op = _sb_attn(qb, ktb, vb, sb_logit_bias[j], tri_sb, batch)
            kp_l.append(jnp.transpose(kf, (0, 3, 1, 2)))
            vp_l.append(jnp.transpose(vf, (0, 3, 1, 2)))
            qkv_s = _rms_matmul(xs, g[1], w_qkv)
            ks_l.append(qkv_s[:, d:2 * d].reshape(dec_batch, 1, sb_heads, hd))
            vs_l.append(qkv_s[:, 2 * d:].reshape(dec_batch, 1, sb_heads, hd))
            bias_rep = jnp.broadcast_to(sb_logit_bias[j][:, None], (sb_heads, page))
            os_ = _sb_decode(qkv_s[:, :d], pool_kt, pool_vt, j, page_table, bias_rep,
                             tri_dec, hd ** -0.5).reshape(dec_batch, d)
        else:
            w_in = jnp.pad(ml_w_in[j], ((0, 0), (0, gate_pad))).astype(BF16)
            w_o = ml_w_out[j].astype(BF16)
            bg_row = jnp.pad(ml_b_gates[j], (0, gate_pad)).reshape(1, 128)
            hg = ml_head_g[j].reshape(ml_heads, 1, dv)
            og, gates, qb, ktb, vb = _ml_proj(xp, g[1], w_in, ml_heads, dqk, dv, ml_tile)
            op, cx, mm = _ml_chunk(qb, ktb, vb, og, gates, bg_row, hg, tril_ml, batch)
            cp_l.append(cx[..., :dv])
            np_l.append(cx[..., dv])
            mp_l.append(mm[:, :, 0, 0])
            proj_s = _rms_matmul(xs, g[1], w_in)
            os_, c_s, n_s, m_s = _ml_step(proj_s, state_C[j], state_n[j], state_m[j], bg_row, hg,
                                          ml_heads, dqk, dv)
            cs_l.append(c_s)
            ns_l.append(n_s)
            ms_l.append(m_s)
        w_gate, w_proj = ple_w_gate[i].astype(BF16), ple_w_proj[i].astype(BF16)
        g_final = final_row if i == depth - 1 else None
        xp = _post_mixer(op, xp, pp[i], g[2], g[3], w_o, w_in_b, w_out_b, w_gate, w_proj, g_final)
        xs = _post_mixer(os_, xs, ps[i], g[2], g[3], w_o, w_in_b, w_out_b, w_gate, w_proj, g_final)
    return (xp.reshape(batch, seq, d), xs.reshape(dec_batch, 1, d),
            jnp.stack(kp_l), jnp.stack(vp_l), jnp.stack(cp_l), jnp.stack(np_l), jnp.stack(mp_l),
            jnp.stack(ks_l), jnp.stack(vs_l), jnp.stack(cs_l), jnp.stack(ns_l), jnp.stack(ms_l))
```

```python
import functools

import jax
import jax.numpy as jnp
from jax import lax
from jax.experimental import pallas as pl
from jax.experimental.pallas import tpu as pltpu

F32 = jnp.float32
BF16 = jnp.bfloat16

RMS_EPS = 1e-6
LOG2E = 1.4426950408889634
MXU_COLS = 256
VMEM_LIMIT_BYTES = 56 * 1024 * 1024
ROW_TILE = 512
SB_TILE = 256
SB_HEADS_PER_STEP = 4
ML_TILE = 256
PAGES_PER_STEP = 16


def _cparams(*sem):
    return pltpu.CompilerParams(dimension_semantics=sem, vmem_limit_bytes=VMEM_LIMIT_BYTES)


def _const_spec(shape):
    nd = len(shape)
    return pl.BlockSpec(shape, lambda *_: (0,) * nd, pipeline_mode=pl.Buffered(1))


def _layer(w, *idx):
    return (w, idx)


def _wshape(wl):
    w, idx = wl
    return w.shape[len(idx):]


def _wspec(wl):
    w, idx = wl
    shape = w.shape[len(idx):]
    return pl.BlockSpec((None,) * len(idx) + shape, lambda *_: idx + (0,) * len(shape),
                        pipeline_mode=pl.Buffered(1))


def _dot(a, b):
    return jnp.dot(a, b, preferred_element_type=F32)


def _rms(x, g):
    return x * lax.rsqrt(jnp.mean(x * x, axis=-1, keepdims=True) + RMS_EPS) * g


def _sigmoid(x):
    return 1.0 / (1.0 + jnp.exp(-x))


def _softplus(z):
    return jnp.maximum(z, 0.0) + jnp.log(1.0 + jnp.exp2(jnp.abs(z) * (-LOG2E)))


def _softplus_log2(z2):
    return jnp.maximum(z2, 0.0) + jnp.log(1.0 + jnp.exp2(-jnp.abs(z2))) * LOG2E


def _split_bf16(a, parts):
    out = []
    for _ in range(parts - 1):
        hi = a.astype(BF16)
        out.append(hi)
        a = a - hi.astype(F32)
    out.append(a.astype(BF16))
    return out


def _dot_split_lhs(a, b01, parts):
    acc = None
    for p in _split_bf16(a, parts):
        y = _dot(p, b01)
        acc = y if acc is None else acc + y
    return acc


def _dot_split_rhs(a01, b, parts):
    acc = None
    for p in _split_bf16(b, parts):
        y = _dot(a01, p)
        acc = y if acc is None else acc + y
    return acc


def _col_chunks(n, max_cols):
    assert n % MXU_COLS == 0
    step = max(MXU_COLS, (max_cols // MXU_COLS) * MXU_COLS)
    return [(c, min(c + step, n)) for c in range(0, n, step)]


def _swiglu(h, win_ref, wout_ref, d_ff, chunks):
    acc = None
    for c0, c1 in chunks:
        gate = _dot(h, win_ref[:, c0:c1])
        up = _dot(h, win_ref[:, d_ff + c0:d_ff + c1])
        act = (gate * _sigmoid(gate) * up).astype(BF16)
        y = _dot(act, wout_ref[c0:c1, :])
        acc = y if acc is None else acc + y
    return acc


def _ffn_kernel(x_ref, g_ref, win_ref, wout_ref, o_ref, *, d_ff, chunks):
    x = x_ref[...]
    o_ref[...] = x + 0.5 * _swiglu(_rms(x, g_ref[...]).astype(BF16), win_ref, wout_ref, d_ff, chunks)


def _ffn(x, g, w_in, w_out):
    m, d = x.shape
    d_ff = _wshape(w_out)[0]
    tm = min(ROW_TILE, m)
    kern = functools.partial(_ffn_kernel, d_ff=d_ff, chunks=_col_chunks(d_ff, 1536))
    return pl.pallas_call(
        kern, grid=(m // tm,),
        in_specs=[pl.BlockSpec((tm, d), lambda i: (i, 0)),
                  _const_spec((1, d)), _wspec(w_in), _wspec(w_out)],
        out_specs=pl.BlockSpec((tm, d), lambda i: (i, 0)),
        out_shape=jax.ShapeDtypeStruct((m, d), F32),
        compiler_params=_cparams("arbitrary"), name="ffn",
    )(x, g, w_in[0], w_out[0])


def _rms_matmul_kernel(x_ref, g_ref, w_ref, o_ref):
    h = _rms(x_ref[...], g_ref[...]).astype(BF16)
    o_ref[...] = _dot(h, w_ref[...])


def _rms_matmul(x, g, w):
    m, d = x.shape
    n = _wshape(w)[1]
    return pl.pallas_call(
        _rms_matmul_kernel, grid=(1,),
        in_specs=[_const_spec((m, d)), _const_spec((1, d)), _wspec(w)],
        out_specs=pl.BlockSpec((m, n), lambda i: (0, 0)),
        out_shape=jax.ShapeDtypeStruct((m, n), F32),
        compiler_params=_cparams("arbitrary"), name="rms_matmul",
    )(x, g, w[0])


def _head_major_stores(q, kt, v, qb_ref, ktb_ref, vb_ref, *, heads, dqk, dv, tk):
    tm = q.shape[0]
    for h in range(heads):
        qb_ref[h] = q[:, h * dqk:(h + 1) * dqk].astype(BF16)
        vb_ref[h] = v[:, h * dv:(h + 1) * dv].astype(BF16)
        for r in range(tm // tk):
            ktb_ref[h, r] = kt[h * dqk:(h + 1) * dqk, r * tk:(r + 1) * tk].astype(BF16)


def _sb_proj_kernel(x_ref, g_ref, w_ref, *refs, heads, hd, tk, n_prev):
    if n_prev:
        kprev_ref, vprev_ref = refs[:2]
        refs = refs[2:]
    kf_ref, vf_ref, qb_ref, ktb_ref, vb_ref = refs
    d = heads * hd
    h = _rms(x_ref[...], g_ref[...]).astype(BF16)
    q = _dot(h, w_ref[:, 0:d]) * (hd ** -0.5 * LOG2E)
    k = _dot(h, w_ref[:, d:2 * d])
    v = _dot(h, w_ref[:, 2 * d:3 * d])
    kt = k.T
    if n_prev:
        kf_ref[0:n_prev] = kprev_ref[...]
        vf_ref[0:n_prev] = vprev_ref[...]
    kf_ref[n_prev, 0] = kt.reshape(heads, hd, kt.shape[1])
    vf_ref[n_prev, 0] = v.T.reshape(heads, hd, kt.shape[1])
    _head_major_stores(q, kt, v, qb_ref, ktb_ref, vb_ref, heads=heads, dqk=hd, dv=hd, tk=tk)


def _sb_proj(x, g, w, heads, tk, batch, prev=None):
    m, d = x.shape
    hd = d // heads
    seq = m // batch
    tm = min(ROW_TILE, seq)
    spb = seq // tm
    n_prev = 0 if prev is None else prev[0].shape[0]
    kv_spec = lambda n: pl.BlockSpec((n, 1, heads, hd, tm), lambda i: (0, i // spb, 0, 0, i % spb))
    kv_shape = jax.ShapeDtypeStruct((n_prev + 1, batch, heads, hd, seq), F32)
    kern = functools.partial(_sb_proj_kernel, heads=heads, hd=hd, tk=tk, n_prev=n_prev)
    return pl.pallas_call(
        kern, grid=(m // tm,),
        in_specs=[pl.BlockSpec((tm, d), lambda i: (i, 0)), _const_spec((1, d)), _wspec(w)]
                 + ([kv_spec(n_prev)] * 2 if n_prev else []),
        out_specs=[kv_spec(n_prev + 1), kv_spec(n_prev + 1),
                   pl.BlockSpec((heads, tm, hd), lambda i: (0, i, 0)),
                   pl.BlockSpec((heads, tm // tk, hd, tk), lambda i: (0, i, 0, 0)),
                   pl.BlockSpec((heads, tm, hd), lambda i: (0, i, 0))],
        out_shape=[kv_shape, kv_shape,
                   jax.ShapeDtypeStruct((heads, m, hd), BF16),
                   jax.ShapeDtypeStruct((heads, m // tk, hd, tk), BF16),
                   jax.ShapeDtypeStruct((heads, m, hd), BF16)],
        compiler_params=_cparams("arbitrary"), name="sb_proj",
    )(x, g, w[0], *(prev or ()))


def _ml_proj_kernel(x_ref, g_ref, w_ref, og_ref, gt_ref, qb_ref, ktb_ref, vb_ref,
                    *, heads, dqk, dv, tk):
    hq, hv = heads * dqk, heads * dv
    h = _rms(x_ref[...], g_ref[...]).astype(BF16)
    q = _dot(h, w_ref[:, 0:hq]) * (dqk ** -0.5)
    k = _dot(h, w_ref[:, hq:2 * hq])
    v = _dot(h, w_ref[:, 2 * hq:2 * hq + hv])
    og_ref[...] = _dot(h, w_ref[:, 2 * hq + hv:2 * hq + 2 * hv])
    gt_ref[...] = _dot(h, w_ref[:, 2 * hq + 2 * hv:])
    _head_major_stores(q, k.T, v, qb_ref, ktb_ref, vb_ref, heads=heads, dqk=dqk, dv=dv, tk=tk)


def _ml_proj(x, g, w, heads, dqk, dv, tk):
    m, d = x.shape
    hv = heads * dv
    ng = _wshape(w)[1] - 2 * heads * dqk - 2 * hv
    tm = min(ROW_TILE, m)
    kern = functools.partial(_ml_proj_kernel, heads=heads, dqk=dqk, dv=dv, tk=tk)
    return pl.pallas_call(
        kern, grid=(m // tm,),
        in_specs=[pl.BlockSpec((tm, d), lambda i: (i, 0)), _const_spec((1, d)), _wspec(w)],
        out_specs=[pl.BlockSpec((tm, hv), lambda i: (i, 0)),
                   pl.BlockSpec((tm, ng), lambda i: (i, 0)),
                   pl.BlockSpec((heads, tm, dqk), lambda i: (0, i, 0)),
                   pl.BlockSpec((heads, tm // tk, dqk, tk), lambda i: (0, i, 0, 0)),
                   pl.BlockSpec((heads, tm, dv), lambda i: (0, i, 0))],
        out_shape=[jax.ShapeDtypeStruct((m, hv), F32), jax.ShapeDtypeStruct((m, ng), F32),
                   jax.ShapeDtypeStruct((heads, m, dqk), BF16),
                   jax.ShapeDtypeStruct((heads, m // tk, dqk, tk), BF16),
                   jax.ShapeDtypeStruct((heads, m, dv), BF16)],
        compiler_params=_cparams("arbitrary"), name="ml_proj",
    )(x, g, w[0])


def _post_mixer_kernel(a_ref, x_ref, p_ref, g2_ref, g3_ref, wo_ref, win_ref, wout_ref, wg_ref, wp_ref,
                       *refs, d_ff, chunks, final):
    x = x_ref[...] + _dot(a_ref[...].astype(BF16), wo_ref[...])
    x = x + 0.5 * _swiglu(_rms(x, g2_ref[...]).astype(BF16), win_ref, wout_ref, d_ff, chunks)
    gate = _sigmoid(_dot(_rms(x, g3_ref[...]).astype(BF16), wg_ref[...]))
    out = x + gate * _dot(p_ref[...].astype(BF16), wp_ref[...])
    if final:
        gf_ref, y_ref = refs
        y_ref[...] = _rms(out, gf_ref[...])
    else:
        refs[0][...] = out


def _post_mixer(a, x, p, g2, g3, w_o, w_in, w_out, w_gate, w_proj, g_final=None):
    m, d = x.shape
    ka, dp, d_ff = a.shape[1], p.shape[1], _wshape(w_out)[0]
    tm = min(ROW_TILE, m)
    row = lambda n: pl.BlockSpec((tm, n), lambda i: (i, 0))
    final = g_final is not None
    extra_specs, extra_args = ([_const_spec((1, d))], [g_final]) if final else ([], [])
    kern = functools.partial(_post_mixer_kernel, d_ff=d_ff, chunks=_col_chunks(d_ff, 1536), final=final)
    return pl.pallas_call(
        kern, grid=(m // tm,),
        in_specs=[row(ka), row(d), row(dp), _const_spec((1, d)), _const_spec((1, d)),
                  _wspec(w_o), _wspec(w_in), _wspec(w_out), _wspec(w_gate), _wspec(w_proj)] + extra_specs,
        out_specs=row(d),
        out_shape=jax.ShapeDtypeStruct((m, d), F32),
        compiler_params=_cparams("arbitrary"), name="post_mixer",
    )(a, x, p, g2, g3, w_o[0], w_in[0], w_out[0], w_gate[0], w_proj[0], *extra_args)


MASKED_LOGIT = -1e30


def _sb_attn_kernel(bias_ref, q_ref, kt_ref, v_ref, tri_ref, o_ref,
                    rsum_ref, acc_ref, z_ref, hi_ref, lo_ref, *, tq, heads_per_step):
    hp = pl.program_id(1)
    i = pl.program_id(2)
    tri = tri_ref[...]

    all_heads = range(heads_per_step)

    def scores(t, slot, masked=False, heads=all_heads):
        for hh in heads:
            z = _dot(q_ref[hh], kt_ref[hh, i - t]) + bias_ref[hp * heads_per_step + hh] * LOG2E
            sp = _softplus_log2(z)
            if masked:
                row = lax.broadcasted_iota(jnp.int32, (tq, tq), 0)
                col = lax.broadcasted_iota(jnp.int32, (tq, tq), 1)
                sp = jnp.where(col < row, sp, 0.0)
                z = jnp.where(col < row, z, MASKED_LOGIT)
            hi, lo = _split_bf16(sp, 2)
            z_ref[slot, hh] = z
            hi_ref[slot, hh] = hi
            lo_ref[slot, hh] = lo

    def weights(t, slot, heads=all_heads):
        start = pl.multiple_of((i - t) * tq, tq)
        for hh in heads:
            inc_hi = _dot(hi_ref[slot, hh], tri)
            inc_lo = _dot(lo_ref[slot, hh], tri)
            rsum = rsum_ref[hh]
            base = z_ref[slot, hh] - jnp.concatenate([rsum] * (tq // 128), axis=1)
            a = jnp.exp2((base - inc_hi) - inc_lo)
            acc_ref[hh] += _dot(a.astype(BF16), v_ref[hh, pl.ds(start, tq), :])
            rsum_ref[hh] = rsum + jnp.broadcast_to(inc_hi[:, 0:1] + inc_lo[:, 0:1], rsum.shape)

    rsum_ref[...] = jnp.zeros_like(rsum_ref)
    acc_ref[...] = jnp.zeros_like(acc_ref)
    scores(0, 0, masked=True)

    def pair(p, carry):
        t = 2 * p + 1
        for hh in all_heads:
            weights(t - 1, 0, [hh])
            scores(t, 1, heads=[hh])
        for hh in all_heads:
            weights(t, 1, [hh])
            scores(t + 1, 0, heads=[hh])
        return carry

    lax.fori_loop(0, i // 2, pair, 0)

    @pl.when(i % 2 == 1)
    def _():
        for hh in all_heads:
            weights(i - 1, 0, [hh])
            scores(i, 1, heads=[hh])
        weights(i, 1)

    @pl.when(i % 2 == 0)
    def _():
        weights(i, 0)

    o_ref[...] = jnp.concatenate([acc_ref[hh] for hh in range(heads_per_step)], axis=1).astype(BF16)


def _sb_attn(qb, ktb, vb, bias, tri, batch):
    heads, m, hd = qb.shape
    seq = m // batch
    tq = ktb.shape[3]
    nq = seq // tq
    hps = SB_HEADS_PER_STEP
    kern = functools.partial(_sb_attn_kernel, tq=tq, heads_per_step=hps)
    return pl.pallas_call(
        kern, grid=(batch, heads // hps, nq),
        in_specs=[pl.BlockSpec(memory_space=pltpu.SMEM),
                  pl.BlockSpec((hps, tq, hd), lambda b, h, i: (h, b * nq + i, 0)),
                  pl.BlockSpec((hps, nq, hd, tq), lambda b, h, i: (h, b, 0, 0)),
                  pl.BlockSpec((hps, seq, hd), lambda b, h, i: (h, b, 0)),
                  _const_spec(tri.shape)],
        out_specs=pl.BlockSpec((tq, hps * hd), lambda b, h, i: (b * nq + i, h)),
        out_shape=jax.ShapeDtypeStruct((m, heads * hd), BF16),
        scratch_shapes=[pltpu.VMEM((hps, tq, 128), F32), pltpu.VMEM((hps, tq, hd), F32),
                        pltpu.VMEM((2, hps, tq, tq), F32), pltpu.VMEM((2, hps, tq, tq), BF16),
                        pltpu.VMEM((2, hps, tq, tq), BF16)],
        compiler_params=_cparams("arbitrary", "arbitrary", "arbitrary"), name="sb_attn",
    )(bias, qb, ktb, vb, tri)


def _sb_decode_kernel(pt_ref, q_ref, bias_ref, tri_ref, *refs, pages, scale):
    del pt_ref
    k_refs, v_refs = refs[:pages], refs[pages:2 * pages]
    o_ref, qrep_ref, rsum_ref, acc_ref = refs[2 * pages:]
    s = pl.program_id(1)
    heads, hd, page = k_refs[0].shape

    @pl.when(s == 0)
    def _():
        qrow = q_ref[0] * scale
        qrep_ref[...] = jnp.broadcast_to(qrow, (page, heads * hd)).T.reshape(heads, hd, page)
        rsum_ref[...] = jnp.zeros_like(rsum_ref)
        acc_ref[...] = jnp.zeros_like(acc_ref)

    qrep = qrep_ref[...]
    bias = bias_ref[...]
    z = jnp.concatenate([jnp.sum(k_refs[i][...] * qrep, axis=1) + bias for i in range(pages)], axis=0)
    sp = _softplus(z)
    inc = _dot_split_lhs(sp, tri_ref[...], 2)
    rsum = rsum_ref[...]
    for i in reversed(range(pages)):
        rows = slice(i * heads, (i + 1) * heads)
        a = jnp.exp(z[rows] - inc[rows] - rsum)
        rsum = rsum + jnp.broadcast_to(inc[rows, 0:1], rsum.shape)
        for h in range(heads):
            acc_ref[h] += v_refs[i][h] * a[h:h + 1, :]
    rsum_ref[...] = rsum

    @pl.when(s == pl.num_programs(1) - 1)
    def _():
        o_ref[0] = jnp.sum(acc_ref[...], axis=2)


def _sb_decode(q, pool_kt, pool_vt, layer, page_table, bias_rep, tri, scale):
    b, d = q.shape
    n_pages = page_table.shape[1]
    heads, hd, page = pool_kt.shape[2:]
    pages = min(PAGES_PER_STEP, n_pages)
    steps = n_pages // pages

    def page_spec(i):
        return pl.BlockSpec((None, None, heads, hd, page),
                            lambda bb, s, pt: (layer, pt[bb, (steps - 1 - s) * pages + i], 0, 0, 0))

    const = lambda shape: pl.BlockSpec(shape, lambda bb, s, pt: (0,) * len(shape))
    kern = functools.partial(_sb_decode_kernel, pages=pages, scale=scale)
    return pl.pallas_call(
        kern,
        grid_spec=pltpu.PrefetchScalarGridSpec(
            num_scalar_prefetch=1, grid=(b, steps),
            in_specs=[pl.BlockSpec((1, 1, d), lambda bb, s, pt: (bb, 0, 0)),
                      const(bias_rep.shape), const(tri.shape)]
                     + [page_spec(i) for i in range(pages)] * 2,
            out_specs=pl.BlockSpec((1, heads, hd), lambda bb, s, pt: (bb, 0, 0)),
            scratch_shapes=[pltpu.VMEM((heads, hd, page), F32), pltpu.VMEM((heads, page), F32),
                            pltpu.VMEM((heads, hd, page), F32)]),
        out_shape=jax.ShapeDtypeStruct((b, heads, hd), F32),
        compiler_params=_cparams("arbitrary", "arbitrary"), name="sb_decode",
    )(page_table, q.reshape(b, 1, d), bias_rep, tri, *([pool_kt] * pages), *([pool_vt] * pages))


def _ml_chunk_kernel(q_ref, kt_ref, v_ref, og_ref, gt_ref, bg_ref, hg_ref, tril_ref,
                     o_ref, cx_out_ref, m_out_ref, cx_ref, m_ref, *, heads, dv):
    c = pl.program_id(1)
    tl = gt_ref.shape[0]

    @pl.when(c == 0)
    def _():
        cx_ref[...] = jnp.zeros_like(cx_ref)
        m_ref[...] = jnp.zeros_like(m_ref)

    gts = gt_ref[...] + bg_ref[...]
    lf = -_softplus(-gts)
    bcum = _dot_split_rhs(tril_ref[...], lf, 3)
    bcum = pltpu.roll(bcum, shift=128 - heads, axis=1)
    u_t = (gts - bcum).T
    row = lax.broadcasted_iota(jnp.int32, (tl, tl), 0)
    col = lax.broadcasted_iota(jnp.int32, (tl, tl), 1)
    causal = col <= row
    one_col = (lax.broadcasted_iota(jnp.int32, (tl, dv), 1) == 0).astype(F32)

    hs = range(heads)
    bcol = [bcum[:, h:h + 1] for h in hs]
    m_prev = [m_ref[h, 0:1, 0:1] for h in hs]
    cx = [cx_ref[h] for h in hs]
    umat = [jnp.where(causal, u_t[h:h + 1, :], -jnp.inf) for h in hs]
    cm = [jnp.maximum(m_prev[h], jnp.max(umat[h], axis=1, keepdims=True)) for h in hs]
    sqk = [_dot(q_ref[h], kt_ref[h, 0]) for h in hs]
    qc = [_dot(q_ref[h], cx[h].astype(BF16)) for h in hs]
    w = [jnp.exp(umat[h] - cm[h]) * sqk[h] for h in hs]
    g = [jnp.exp(m_prev[h] - cm[h]) for h in hs]
    m_t = [bcol[h] + cm[h] for h in hs]
    num = [_dot(w[h].astype(BF16), v_ref[h]) + g[h] * qc[h][:, :dv] for h in hs]
    den = [jnp.sum(w[h], axis=1, keepdims=True) + g[h] * qc[h][:, dv:dv + 1] for h in hs]
    hout = [num[h] * (1.0 / jnp.maximum(jnp.abs(den[h]), jnp.exp(-m_t[h]))) for h in hs]
    ms = [jnp.mean(hout[h] * hout[h], axis=1, keepdims=True) for h in hs]
    for h in hs:
        hn = hout[h] * lax.rsqrt(ms[h] + RMS_EPS) * hg_ref[h]
        o_ref[:, h * dv:(h + 1) * dv] = (_sigmoid(og_ref[:, h * dv:(h + 1) * dv]) * hn).astype(BF16)
    for h in hs:
        m_new = m_t[h][tl - 1:tl, :]
        b_end = bcol[h][tl - 1:tl, :]
        wk = jnp.exp(b_end - bcol[h] + gts[:, h:h + 1] - m_new)
        g_end = jnp.exp(b_end + m_prev[h] - m_new)
        vx = (jnp.concatenate([v_ref[h].astype(F32), one_col], axis=1) * wk).astype(BF16)
        cx_ref[h] = g_end * cx[h] + _dot(kt_ref[h, 0], vx)
        m_ref[h] = jnp.broadcast_to(m_new, m_ref.shape[1:])

    @pl.when(c == pl.num_programs(1) - 1)
    def _():
        cx_out_ref[0] = cx_ref[...]
        m_out_ref[0] = m_ref[...]


def _ml_chunk(qb, ktb, vb, og, gates, b_gates_row, head_g, tril, batch):
    heads, m, dqk = qb.shape
    dv = vb.shape[2]
    tl = ktb.shape[3]
    nc = m // batch // tl
    ng = gates.shape[1]
    kern = functools.partial(_ml_chunk_kernel, heads=heads, dv=dv)
    return pl.pallas_call(
        kern, grid=(batch, nc),
        in_specs=[pl.BlockSpec((heads, tl, dqk), lambda b, c: (0, b * nc + c, 0)),
                  pl.BlockSpec((heads, 1, dqk, tl), lambda b, c: (0, b * nc + c, 0, 0)),
                  pl.BlockSpec((heads, tl, dv), lambda b, c: (0, b * nc + c, 0)),
                  pl.BlockSpec((tl, heads * dv), lambda b, c: (b * nc + c, 0)),
                  pl.BlockSpec((tl, ng), lambda b, c: (b * nc + c, 0)),
                  pl.BlockSpec((1, ng), lambda b, c: (0, 0)),
                  pl.BlockSpec((heads, 1, dv), lambda b, c: (0, 0, 0)),
                  pl.BlockSpec((tl, tl), lambda b, c: (0, 0))],
        out_specs=[pl.BlockSpec((tl, heads * dv), lambda b, c: (b * nc + c, 0)),
                   pl.BlockSpec((1, heads, dqk, 2 * dv), lambda b, c: (b, 0, 0, 0)),
                   pl.BlockSpec((1, heads, 8, 128), lambda b, c: (b, 0, 0, 0))],
        out_shape=[jax.ShapeDtypeStruct((m, heads * dv), BF16),
                   jax.ShapeDtypeStruct((batch, heads, dqk, 2 * dv), F32),
                   jax.ShapeDtypeStruct((batch, heads, 8, 128), F32)],
        scratch_shapes=[pltpu.VMEM((heads, dqk, 2 * dv), F32), pltpu.VMEM((heads, 8, 128), F32)],
        compiler_params=_cparams("arbitrary", "arbitrary"), name="ml_chunk",
    )(qb, ktb, vb, og, gates, b_gates_row, head_g, tril)


def _ml_step_kernel(pr_ref, c0_ref, n0_ref, m0_ref, bg_ref, hg_ref,
                    o_ref, c_ref, n_ref, m_ref, *, heads, dqk, dv):
    hq, hv = heads * dqk, heads * dv
    gts = pr_ref[0, :, 2 * hq + 2 * hv:] + bg_ref[...]
    ig = gts[:, 0:heads]
    lf = -_softplus(-gts[:, heads:2 * heads])
    inter = lf + m0_ref[0]
    m_t = jnp.maximum(inter, ig)
    m_ref[0] = m_t
    wgt = jnp.exp(ig - m_t)
    g = jnp.exp(inter - m_t)
    floor = jnp.exp(-m_t)
    eye = (lax.broadcasted_iota(jnp.int32, (dqk, dqk), 0)
           == lax.broadcasted_iota(jnp.int32, (dqk, dqk), 1))

    hs = range(heads)
    q = [pr_ref[0, :, h * dqk:(h + 1) * dqk] * (dqk ** -0.5) for h in hs]
    k = [pr_ref[0, :, hq + h * dqk:hq + (h + 1) * dqk] for h in hs]
    v = [pr_ref[0, :, 2 * hq + h * dv:2 * hq + (h + 1) * dv] for h in hs]
    n0 = [n0_ref[0, h:h + 1, :] for h in hs]
    qk = [jnp.sum(q[h] * k[h], axis=1, keepdims=True) for h in hs]
    qn = [jnp.sum(q[h] * n0[h], axis=1, keepdims=True) for h in hs]
    qc = [_dot(q[h].astype(BF16), c0_ref[0, h].astype(BF16)) for h in hs]
    kv = [_dot(jnp.where(eye, wgt[:, h:h + 1] * k[h], 0.0).astype(BF16),
               jnp.broadcast_to(v[h], (dqk, dv)).astype(BF16)) for h in hs]
    hout = []
    for h in hs:
        w_h, g_h = wgt[:, h:h + 1], g[:, h:h + 1]
        num = (w_h * qk[h]) * v[h] + g_h * qc[h]
        den = w_h * qk[h] + g_h * qn[h]
        hout.append(num * (1.0 / jnp.maximum(jnp.abs(den), floor[:, h:h + 1])))
        c_ref[0, h] = g_h * c0_ref[0, h] + kv[h]
        n_ref[0, h:h + 1, :] = g_h * n0[h] + w_h * k[h]
    ms = [jnp.mean(hout[h] * hout[h], axis=1, keepdims=True) for h in hs]
    for h in hs:
        og = pr_ref[0, :, 2 * hq + hv + h * dv:2 * hq + hv + (h + 1) * dv]
        hn = hout[h] * lax.rsqrt(ms[h] + RMS_EPS) * hg_ref[h]
        o_ref[0, :, h * dv:(h + 1) * dv] = _sigmoid(og) * hn


def _ml_step(proj, c0, n0, m0, b_gates_row, head_g, heads, dqk, dv):
    b, n = proj.shape
    hv = heads * dv
    kern = functools.partial(_ml_step_kernel, heads=heads, dqk=dqk, dv=dv)
    out, c, nn, mm = pl.pallas_call(
        kern, grid=(b,),
        in_specs=[pl.BlockSpec((1, 1, n), lambda i: (i, 0, 0)),
                  pl.BlockSpec((1, heads, dqk, dv), lambda i: (i, 0, 0, 0)),
                  pl.BlockSpec((1, heads, dqk), lambda i: (i, 0, 0)),
                  pl.BlockSpec((1, 1, heads), lambda i: (i, 0, 0)),
                  pl.BlockSpec((1, b_gates_row.shape[1]), lambda i: (0, 0)),
                  pl.BlockSpec((heads, 1, dv), lambda i: (0, 0, 0))],
        out_specs=[pl.BlockSpec((1, 1, hv), lambda i: (i, 0, 0)),
                   pl.BlockSpec((1, heads, dqk, dv), lambda i: (i, 0, 0, 0)),
                   pl.BlockSpec((1, heads, dqk), lambda i: (i, 0, 0)),
                   pl.BlockSpec((1, 1, heads), lambda i: (i, 0, 0))],
        out_shape=[jax.ShapeDtypeStruct((b, 1, hv), F32),
                   jax.ShapeDtypeStruct((b, heads, dqk, dv), F32),
                   jax.ShapeDtypeStruct((b, heads, dqk), F32),
                   jax.ShapeDtypeStruct((b, 1, heads), F32)],
        compiler_params=_cparams("arbitrary"), name="ml_step",
    )(proj.reshape(b, 1, n), c0, n0, m0.reshape(b, 1, heads), b_gates_row, head_g)
    return out.reshape(b, hv), c, nn, mm.reshape(b, heads)


def _suffix_ones(n):
    i = lax.broadcasted_iota(jnp.int32, (n, n), 0)
    j = lax.broadcasted_iota(jnp.int32, (n, n), 1)
    return (i >= j).astype(BF16)


def kernel(x_prompt, x_sample, cache_k, cache_v, state_C, state_n, state_m, page_table, p_prompt, p_sample, norm_g, ffn_w_in, ffn_w_out, sb_w_qkv, sb_w_o, sb_logit_bias, ml_w_in, ml_b_gates, ml_head_g, ml_w_out, ple_w_proj, ple_w_gate, final_norm_g):
    batch, seq, d = x_prompt.shape
    dec_batch = x_sample.shape[0]
    depth = norm_g.shape[0]
    sb_heads = sb_logit_bias.shape[1]
    hd = d // sb_heads
    ml_heads, dv = ml_head_g.shape[1], ml_head_g.shape[2]
    dqk = (ml_w_in.shape[2] - 2 * ml_heads * dv - 2 * ml_heads) // (2 * ml_heads)
    page = cache_k.shape[2]
    mp = batch * seq

    xp = x_prompt.reshape(mp, d)
    xs = x_sample.reshape(dec_batch, d)
    pp = p_prompt.reshape(depth, mp, -1)
    ps = p_sample.reshape(depth, dec_batch, -1)
    pool_kt = jnp.transpose(cache_k, (0, 1, 3, 4, 2))
    pool_vt = jnp.transpose(cache_v, (0, 1, 3, 4, 2))

    sb_tile = min(SB_TILE, seq)
    ml_tile = min(ML_TILE, seq)
    tri_sb = _suffix_ones(sb_tile)
    tril_ml = _suffix_ones(ml_tile)
    tri_dec = _suffix_ones(page)

    gate_pad = 128 - 2 * ml_heads
    norm_rows = norm_g.reshape(depth, 4, 1, d)
    final_row = final_norm_g.reshape(1, d)

    w_ffn_in, w_ffn_out = ffn_w_in.astype(BF16), ffn_w_out.astype(BF16)
    w_sb_qkv, w_sb_o = sb_w_qkv.astype(BF16), sb_w_o.astype(BF16)
    w_ml_in = jnp.pad(ml_w_in, ((0, 0), (0, 0), (0, gate_pad))).astype(BF16)
    w_ml_o = ml_w_out.astype(BF16)
    w_ple_gate, w_ple_proj = ple_w_gate.astype(BF16), ple_w_proj.astype(BF16)

    kv_prompt = None
    ks_l, vs_l = [], []
    cp_l, np_l, mp_l, cs_l, ns_l, ms_l = [], [], [], [], [], []
    for i in range(depth):
        g = norm_rows[i]
        j = i // 2
        xp = _ffn(xp, g[0], _layer(w_ffn_in, i, 0), _layer(w_ffn_out, i, 0))
        xs = _ffn(xs, g[0], _layer(w_ffn_in, i, 0), _layer(w_ffn_out, i, 0))
        if i % 2 == 0:
            w_qkv, w_o = _layer(w_sb_qkv, j), _layer(w_sb_o, j)
            kf, vf, qb, ktb, vb = _sb_proj(xp, g[1], w_qkv, sb_heads, sb_tile, batch, kv_prompt)
            kv_prompt = (kf, vf)
            op = _sb_attn(qb, ktb, vb, sb_logit_bias[j], tri_sb, batch)
            qkv_s = _rms_matmul(xs, g[1], w_qkv)
            ks_l.append(qkv_s[:, d:2 * d].reshape(dec_batch, 1, sb_heads, hd))
            vs_l.append(qkv_s[:, 2 * d:].reshape(dec_batch, 1, sb_heads, hd))
            bias_rep = jnp.broadcast_to(sb_logit_bias[j][:, None], (sb_heads, page))
            os_ = _sb_decode(qkv_s[:, :d], pool_kt, pool_vt, j, page_table, bias_rep,
                             tri_dec, hd ** -0.5).reshape(dec_batch, d)
        else:
            w_in, w_o = _layer(w_ml_in, j), _layer(w_ml_o, j)
            bg_row = jnp.pad(ml_b_gates[j], (0, gate_pad)).reshape(1, 128)
            hg = ml_head_g[j].reshape(ml_heads, 1, dv)
            og, gates, qb, ktb, vb = _ml_proj(xp, g[1], w_in, ml_heads, dqk, dv, ml_tile)
            op, cx, mm = _ml_chunk(qb, ktb, vb, og, gates, bg_row, hg, tril_ml, batch)
            cp_l.append(cx[..., :dv])
            np_l.append(cx[..., dv])
            mp_l.append(mm[:, :, 0, 0])
            proj_s = _rms_matmul(xs, g[1], w_in)
            os_, c_s, n_s, m_s = _ml_step(proj_s, state_C[j], state_n[j], state_m[j], bg_row, hg,
                                          ml_heads, dqk, dv)
            cs_l.append(c_s)
            ns_l.append(n_s)
            ms_l.append(m_s)
        g_final = final_row if i == depth - 1 else None
        post = (g[2], g[3], w_o, _layer(w_ffn_in, i, 1), _layer(w_ffn_out, i, 1),
                _layer(w_ple_gate, i), _layer(w_ple_proj, i), g_final)
        xp = _post_mixer(op, xp, pp[i], *post)
        xs = _post_mixer(os_, xs, ps[i], *post)
    kp, vp = (jnp.transpose(a, (0, 1, 4, 2, 3)) for a in kv_prompt)
    return (xp.reshape(batch, seq, d), xs.reshape(dec_batch, 1, d),
            kp, vp, jnp.stack(cp_l), jnp.stack(np_l), jnp.stack(mp_l),
            jnp.stack(ks_l), jnp.stack(vs_l), jnp.stack(cs_l), jnp.stack(ns_l), jnp.stack(ms_l))
```

```python
import functools

import jax
import jax.numpy as jnp
from jax import lax
from jax.experimental import pallas as pl
from jax.experimental.pallas import tpu as pltpu

F32 = jnp.float32
BF16 = jnp.bfloat16

RMS_EPS = 1e-6
LOG2E = 1.4426950408889634
SOFTPLUS_CLAMP = 100.0
MXU_COLS = 256
VMEM_LIMIT_BYTES = 56 * 1024 * 1024
ROW_TILE = 512
SB_TILE = 256
SB_HEADS_PER_STEP = 4
ML_TILE = 256
PAGES_PER_STEP = 16


def _cparams(*sem):
    return pltpu.CompilerParams(dimension_semantics=sem, vmem_limit_bytes=VMEM_LIMIT_BYTES)


def _const_spec(shape):
    nd = len(shape)
    return pl.BlockSpec(shape, lambda *_: (0,) * nd, pipeline_mode=pl.Buffered(1))


def _layer(w, *idx):
    return (w, idx)


def _wshape(wl):
    w, idx = wl
    return w.shape[len(idx):]


def _wspec(wl):
    w, idx = wl
    shape = w.shape[len(idx):]
    return pl.BlockSpec((None,) * len(idx) + shape, lambda *_: idx + (0,) * len(shape),
                        pipeline_mode=pl.Buffered(1))


def _dot(a, b):
    return jnp.dot(a, b, preferred_element_type=F32)


def _rms(x, g):
    return x * lax.rsqrt(jnp.mean(x * x, axis=-1, keepdims=True) + RMS_EPS) * g


def _sigmoid(x):
    return 1.0 / (1.0 + jnp.exp(-x))


def _softplus(z):
    return jnp.maximum(z, 0.0) + jnp.log(1.0 + jnp.exp2(jnp.abs(z) * (-LOG2E)))


def _softplus_log2(z2):
    return jnp.maximum(jnp.log(1.0 + jnp.exp2(jnp.minimum(z2, SOFTPLUS_CLAMP))) * LOG2E, z2)


def _split_bf16(a, parts):
    out = []
    for _ in range(parts - 1):
        hi = a.astype(BF16)
        out.append(hi)
        a = a - hi.astype(F32)
    out.append(a.astype(BF16))
    return out


def _dot_split_lhs(a, b01, parts):
    acc = None
    for p in _split_bf16(a, parts):
        y = _dot(p, b01)
        acc = y if acc is None else acc + y
    return acc


def _dot_split_rhs(a01, b, parts):
    acc = None
    for p in _split_bf16(b, parts):
        y = _dot(a01, p)
        acc = y if acc is None else acc + y
    return acc


def _col_chunks(n, max_cols):
    assert n % MXU_COLS == 0
    step = max(MXU_COLS, (max_cols // MXU_COLS) * MXU_COLS)
    return [(c, min(c + step, n)) for c in range(0, n, step)]


def _swiglu(h, win_ref, wout_ref, d_ff, chunks):
    acc = None
    for c0, c1 in chunks:
        gate = _dot(h, win_ref[:, c0:c1])
        up = _dot(h, win_ref[:, d_ff + c0:d_ff + c1])
        act = (gate * _sigmoid(gate) * up).astype(BF16)
        y = _dot(act, wout_ref[c0:c1, :])
        acc = y if acc is None else acc + y
    return acc


def _ffn_kernel(x_ref, g_ref, win_ref, wout_ref, o_ref, *, d_ff, chunks):
    x = x_ref[...]
    o_ref[...] = x + 0.5 * _swiglu(_rms(x, g_ref[...]).astype(BF16), win_ref, wout_ref, d_ff, chunks)


def _ffn(x, g, w_in, w_out):
    m, d = x.shape
    d_ff = _wshape(w_out)[0]
    tm = min(ROW_TILE, m)
    kern = functools.partial(_ffn_kernel, d_ff=d_ff, chunks=_col_chunks(d_ff, 1536))
    return pl.pallas_call(
        kern, grid=(m // tm,),
        in_specs=[pl.BlockSpec((tm, d), lambda i: (i, 0)),
                  _const_spec((1, d)), _wspec(w_in), _wspec(w_out)],
        out_specs=pl.BlockSpec((tm, d), lambda i: (i, 0)),
        out_shape=jax.ShapeDtypeStruct((m, d), F32),
        compiler_params=_cparams("arbitrary"), name="ffn",
    )(x, g, w_in[0], w_out[0])


def _rms_matmul_kernel(x_ref, g_ref, w_ref, o_ref):
    h = _rms(x_ref[...], g_ref[...]).astype(BF16)
    o_ref[...] = _dot(h, w_ref[...])


def _rms_matmul(x, g, w):
    m, d = x.shape
    n = _wshape(w)[1]
    return pl.pallas_call(
        _rms_matmul_kernel, grid=(1,),
        in_specs=[_const_spec((m, d)), _const_spec((1, d)), _wspec(w)],
        out_specs=pl.BlockSpec((m, n), lambda i: (0, 0)),
        out_shape=jax.ShapeDtypeStruct((m, n), F32),
        compiler_params=_cparams("arbitrary"), name="rms_matmul",
    )(x, g, w[0])


def _head_major_stores(q, kt, v, qb_ref, ktb_ref, vb_ref, *, heads, dqk, dv, tk):
    tm = q.shape[0]
    for h in range(heads):
        qb_ref[h] = q[:, h * dqk:(h + 1) * dqk].astype(BF16)
        vb_ref[h] = v[:, h * dv:(h + 1) * dv].astype(BF16)
        for r in range(tm // tk):
            ktb_ref[h, r] = kt[h * dqk:(h + 1) * dqk, r * tk:(r + 1) * tk].astype(BF16)


def _sb_proj_kernel(x_ref, g_ref, w_ref, *refs, heads, hd, tk, n_prev):
    if n_prev:
        kprev_ref, vprev_ref = refs[:2]
        refs = refs[2:]
    kf_ref, vf_ref, qb_ref, ktb_ref, vb_ref = refs
    d = heads * hd
    h = _rms(x_ref[...], g_ref[...]).astype(BF16)
    q = _dot(h, w_ref[:, 0:d]) * (hd ** -0.5 * LOG2E)
    k = _dot(h, w_ref[:, d:2 * d])
    v = _dot(h, w_ref[:, 2 * d:3 * d])
    kt = k.T
    if n_prev:
        kf_ref[0:n_prev] = kprev_ref[...]
        vf_ref[0:n_prev] = vprev_ref[...]
    kf_ref[n_prev, 0] = kt.reshape(heads, hd, kt.shape[1])
    vf_ref[n_prev, 0] = v.T.reshape(heads, hd, kt.shape[1])
    _head_major_stores(q, kt, v, qb_ref, ktb_ref, vb_ref, heads=heads, dqk=hd, dv=hd, tk=tk)


def _sb_proj(x, g, w, heads, tk, batch, prev=None):
    m, d = x.shape
    hd = d // heads
    seq = m // batch
    tm = min(ROW_TILE, seq)
    spb = seq // tm
    n_prev = 0 if prev is None else prev[0].shape[0]
    kv_spec = lambda n: pl.BlockSpec((n, 1, heads, hd, tm), lambda i: (0, i // spb, 0, 0, i % spb))
    kv_shape = jax.ShapeDtypeStruct((n_prev + 1, batch, heads, hd, seq), F32)
    kern = functools.partial(_sb_proj_kernel, heads=heads, hd=hd, tk=tk, n_prev=n_prev)
    return pl.pallas_call(
        kern, grid=(m // tm,),
        in_specs=[pl.BlockSpec((tm, d), lambda i: (i, 0)), _const_spec((1, d)), _wspec(w)]
                 + ([kv_spec(n_prev)] * 2 if n_prev else []),
        out_specs=[kv_spec(n_prev + 1), kv_spec(n_prev + 1),
                   pl.BlockSpec((heads, tm, hd), lambda i: (0, i, 0)),
                   pl.BlockSpec((heads, tm // tk, hd, tk), lambda i: (0, i, 0, 0)),
                   pl.BlockSpec((heads, tm, hd), lambda i: (0, i, 0))],
        out_shape=[kv_shape, kv_shape,
                   jax.ShapeDtypeStruct((heads, m, hd), BF16),
                   jax.ShapeDtypeStruct((heads, m // tk, hd, tk), BF16),
                   jax.ShapeDtypeStruct((heads, m, hd), BF16)],
        compiler_params=_cparams("arbitrary"), name="sb_proj",
    )(x, g, w[0], *(prev or ()))


def _ml_proj_kernel(x_ref, g_ref, w_ref, og_ref, gt_ref, qb_ref, ktb_ref, vb_ref,
                    *, heads, dqk, dv, tk):
    hq, hv = heads * dqk, heads * dv
    h = _rms(x_ref[...], g_ref[...]).astype(BF16)
    q = _dot(h, w_ref[:, 0:hq]) * (dqk ** -0.5)
    k = _dot(h, w_ref[:, hq:2 * hq])
    v = _dot(h, w_ref[:, 2 * hq:2 * hq + hv])
    og_ref[...] = _dot(h, w_ref[:, 2 * hq + hv:2 * hq + 2 * hv])
    gt_ref[...] = _dot(h, w_ref[:, 2 * hq + 2 * hv:])
    _head_major_stores(q, k.T, v, qb_ref, ktb_ref, vb_ref, heads=heads, dqk=dqk, dv=dv, tk=tk)


def _ml_proj(x, g, w, heads, dqk, dv, tk):
    m, d = x.shape
    hv = heads * dv
    ng = _wshape(w)[1] - 2 * heads * dqk - 2 * hv
    tm = min(ROW_TILE, m)
    kern = functools.partial(_ml_proj_kernel, heads=heads, dqk=dqk, dv=dv, tk=tk)
    return pl.pallas_call(
        kern, grid=(m // tm,),
        in_specs=[pl.BlockSpec((tm, d), lambda i: (i, 0)), _const_spec((1, d)), _wspec(w)],
        out_specs=[pl.BlockSpec((tm, hv), lambda i: (i, 0)),
                   pl.BlockSpec((tm, ng), lambda i: (i, 0)),
                   pl.BlockSpec((heads, tm, dqk), lambda i: (0, i, 0)),
                   pl.BlockSpec((heads, tm // tk, dqk, tk), lambda i: (0, i, 0, 0)),
                   pl.BlockSpec((heads, tm, dv), lambda i: (0, i, 0))],
        out_shape=[jax.ShapeDtypeStruct((m, hv), F32), jax.ShapeDtypeStruct((m, ng), F32),
                   jax.ShapeDtypeStruct((heads, m, dqk), BF16),
                   jax.ShapeDtypeStruct((heads, m // tk, dqk, tk), BF16),
                   jax.ShapeDtypeStruct((heads, m, dv), BF16)],
        compiler_params=_cparams("arbitrary"), name="ml_proj",
    )(x, g, w[0])


def _post_mixer_kernel(a_ref, x_ref, p_ref, g2_ref, g3_ref, wo_ref, win_ref, wout_ref, wg_ref, wp_ref,
                       *refs, d_ff, chunks, final):
    x = x_ref[...] + _dot(a_ref[...].astype(BF16), wo_ref[...])
    x = x + 0.5 * _swiglu(_rms(x, g2_ref[...]).astype(BF16), win_ref, wout_ref, d_ff, chunks)
    gate = _sigmoid(_dot(_rms(x, g3_ref[...]).astype(BF16), wg_ref[...]))
    out = x + gate * _dot(p_ref[...].astype(BF16), wp_ref[...])
    if final:
        gf_ref, y_ref = refs
        y_ref[...] = _rms(out, gf_ref[...])
    else:
        refs[0][...] = out


def _post_mixer(a, x, p, g2, g3, w_o, w_in, w_out, w_gate, w_proj, g_final=None):
    m, d = x.shape
    ka, dp, d_ff = a.shape[1], p.shape[1], _wshape(w_out)[0]
    tm = min(ROW_TILE, m)
    row = lambda n: pl.BlockSpec((tm, n), lambda i: (i, 0))
    final = g_final is not None
    extra_specs, extra_args = ([_const_spec((1, d))], [g_final]) if final else ([], [])
    kern = functools.partial(_post_mixer_kernel, d_ff=d_ff, chunks=_col_chunks(d_ff, 1536), final=final)
    return pl.pallas_call(
        kern, grid=(m // tm,),
        in_specs=[row(ka), row(d), row(dp), _const_spec((1, d)), _const_spec((1, d)),
                  _wspec(w_o), _wspec(w_in), _wspec(w_out), _wspec(w_gate), _wspec(w_proj)] + extra_specs,
        out_specs=row(d),
        out_shape=jax.ShapeDtypeStruct((m, d), F32),
        compiler_params=_cparams("arbitrary"), name="post_mixer",
    )(a, x, p, g2, g3, w_o[0], w_in[0], w_out[0], w_gate[0], w_proj[0], *extra_args)


MASKED_LOGIT = -1e30


NEXT_SLOT = 2


def _sb_attn_kernel(bias_ref, q_ref, kt_ref, v_ref, tri_ref, o_ref,
                    rsum_ref, acc_ref, z_ref, hi_ref, lo_ref, *, tq, heads_per_step):
    hp = pl.program_id(1)
    i = pl.program_id(2)
    has_next = i + 1 < pl.num_programs(2)
    tri = tri_ref[...]

    all_heads = range(heads_per_step)

    def scores(qi, t, slot, masked=False, heads=all_heads):
        q0 = pl.multiple_of(qi * tq, tq)
        for hh in heads:
            z = (_dot(q_ref[hh, pl.ds(q0, tq), :], kt_ref[hh, qi - t])
                 + bias_ref[hp * heads_per_step + hh] * LOG2E)
            sp = _softplus_log2(z)
            if masked:
                row = lax.broadcasted_iota(jnp.int32, (tq, tq), 0)
                col = lax.broadcasted_iota(jnp.int32, (tq, tq), 1)
                sp = jnp.where(col < row, sp, 0.0)
                z = jnp.where(col < row, z, MASKED_LOGIT)
            hi, lo = _split_bf16(sp, 2)
            z_ref[slot, hh] = z
            hi_ref[slot, hh] = hi
            lo_ref[slot, hh] = lo

    def weights(t, slot, heads=all_heads):
        start = pl.multiple_of((i - t) * tq, tq)
        for hh in heads:
            inc_hi = _dot(hi_ref[slot, hh], tri)
            inc_lo = _dot(lo_ref[slot, hh], tri)
            rsum = rsum_ref[hh]
            base = z_ref[slot, hh] - jnp.concatenate([rsum] * (tq // 128), axis=1)
            a = jnp.exp2((base - inc_hi) - inc_lo)
            acc_ref[hh] += _dot(a.astype(BF16), v_ref[hh, pl.ds(start, tq), :])
            rsum_ref[hh] = rsum + jnp.broadcast_to(inc_hi[:, 0:1] + inc_lo[:, 0:1], rsum.shape)

    def last_weights(slot):
        @pl.when(has_next)
        def _():
            for hh in all_heads:
                weights(i, slot, [hh])
                scores(i + 1, 0, NEXT_SLOT, masked=True, heads=[hh])

        @pl.when(jnp.logical_not(has_next))
        def _():
            weights(i, slot)

    rsum_ref[...] = jnp.zeros_like(rsum_ref)
    acc_ref[...] = jnp.zeros_like(acc_ref)

    @pl.when(i == 0)
    def _():
        scores(i, 0, 0, masked=True)

    @pl.when(i > 0)
    def _():
        z_ref[0] = z_ref[NEXT_SLOT]
        hi_ref[0] = hi_ref[NEXT_SLOT]
        lo_ref[0] = lo_ref[NEXT_SLOT]

    def pair(p, carry):
        t = 2 * p + 1
        for hh in all_heads:
            weights(t - 1, 0, [hh])
            scores(i, t, 1, heads=[hh])
        for hh in all_heads:
            weights(t, 1, [hh])
            scores(i, t + 1, 0, heads=[hh])
        return carry

    lax.fori_loop(0, i // 2, pair, 0)

    @pl.when(i % 2 == 1)
    def _():
        for hh in all_heads:
            weights(i - 1, 0, [hh])
            scores(i, i, 1, heads=[hh])
        last_weights(1)

    @pl.when(i % 2 == 0)
    def _():
        last_weights(0)

    o_ref[...] = jnp.concatenate([acc_ref[hh] for hh in range(heads_per_step)], axis=1).astype(BF16)


def _sb_attn(qb, ktb, vb, bias, tri, batch):
    heads, m, hd = qb.shape
    seq = m // batch
    tq = ktb.shape[3]
    nq = seq // tq
    hps = SB_HEADS_PER_STEP
    kern = functools.partial(_sb_attn_kernel, tq=tq, heads_per_step=hps)
    return pl.pallas_call(
        kern, grid=(batch, heads // hps, nq),
        in_specs=[pl.BlockSpec(memory_space=pltpu.SMEM),
                  pl.BlockSpec((hps, seq, hd), lambda b, h, i: (h, b, 0)),
                  pl.BlockSpec((hps, nq, hd, tq), lambda b, h, i: (h, b, 0, 0)),
                  pl.BlockSpec((hps, seq, hd), lambda b, h, i: (h, b, 0)),
                  _const_spec(tri.shape)],
        out_specs=pl.BlockSpec((tq, hps * hd), lambda b, h, i: (b * nq + i, h)),
        out_shape=jax.ShapeDtypeStruct((m, heads * hd), BF16),
        scratch_shapes=[pltpu.VMEM((hps, tq, 128), F32), pltpu.VMEM((hps, tq, hd), F32),
                        pltpu.VMEM((3, hps, tq, tq), F32), pltpu.VMEM((3, hps, tq, tq), BF16),
                        pltpu.VMEM((3, hps, tq, tq), BF16)],
        compiler_params=_cparams("arbitrary", "arbitrary", "arbitrary"), name="sb_attn",
    )(bias, qb, ktb, vb, tri)


def _sb_decode_kernel(pt_ref, q_ref, bias_ref, tri_ref, *refs, pages, scale):
    del pt_ref
    k_refs, v_refs = refs[:pages], refs[pages:2 * pages]
    o_ref, qrep_ref, rsum_ref, acc_ref = refs[2 * pages:]
    s = pl.program_id(1)
    heads, hd, page = k_refs[0].shape

    @pl.when(s == 0)
    def _():
        qrow = q_ref[0] * scale
        qrep_ref[...] = jnp.broadcast_to(qrow, (page, heads * hd)).T.reshape(heads, hd, page)
        rsum_ref[...] = jnp.zeros_like(rsum_ref)
        acc_ref[...] = jnp.zeros_like(acc_ref)

    qrep = qrep_ref[...]
    bias = bias_ref[...]
    z = jnp.concatenate([jnp.sum(k_refs[i][...] * qrep, axis=1) + bias for i in range(pages)], axis=0)
    sp = _softplus(z)
    inc = _dot_split_lhs(sp, tri_ref[...], 2)
    rsum = rsum_ref[...]
    for i in reversed(range(pages)):
        rows = slice(i * heads, (i + 1) * heads)
        a = jnp.exp(z[rows] - inc[rows] - rsum)
        rsum = rsum + jnp.broadcast_to(inc[rows, 0:1], rsum.shape)
        for h in range(heads):
            acc_ref[h] += v_refs[i][h] * a[h:h + 1, :]
    rsum_ref[...] = rsum

    @pl.when(s == pl.num_programs(1) - 1)
    def _():
        o_ref[0] = jnp.sum(acc_ref[...], axis=2)


def _sb_decode(q, pool_kt, pool_vt, layer, page_table, bias_rep, tri, scale):
    b, d = q.shape
    n_pages = page_table.shape[1]
    heads, hd, page = pool_kt.shape[2:]
    pages = min(PAGES_PER_STEP, n_pages)
    steps = n_pages // pages

    def page_spec(i):
        return pl.BlockSpec((None, None, heads, hd, page),
                            lambda bb, s, pt: (layer, pt[bb, (steps - 1 - s) * pages + i], 0, 0, 0))

    const = lambda shape: pl.BlockSpec(shape, lambda bb, s, pt: (0,) * len(shape))
    kern = functools.partial(_sb_decode_kernel, pages=pages, scale=scale)
    return pl.pallas_call(
        kern,
        grid_spec=pltpu.PrefetchScalarGridSpec(
            num_scalar_prefetch=1, grid=(b, steps),
            in_specs=[pl.BlockSpec((1, 1, d), lambda bb, s, pt: (bb, 0, 0)),
                      const(bias_rep.shape), const(tri.shape)]
                     + [page_spec(i) for i in range(pages)] * 2,
            out_specs=pl.BlockSpec((1, heads, hd), lambda bb, s, pt: (bb, 0, 0)),
            scratch_shapes=[pltpu.VMEM((heads, hd, page), F32), pltpu.VMEM((heads, page), F32),
                            pltpu.VMEM((heads, hd, page), F32)]),
        out_shape=jax.ShapeDtypeStruct((b, heads, hd), F32),
        compiler_params=_cparams("arbitrary", "arbitrary"), name="sb_decode",
    )(page_table, q.reshape(b, 1, d), bias_rep, tri, *([pool_kt] * pages), *([pool_vt] * pages))


def _ml_chunk_kernel(q_ref, kt_ref, v_ref, og_ref, gt_ref, bg_ref, hg_ref, tril_ref,
                     o_ref, cx_out_ref, m_out_ref, cx_ref, m_ref, *, heads, dv):
    c = pl.program_id(1)
    tl = gt_ref.shape[0]

    @pl.when(c == 0)
    def _():
        cx_ref[...] = jnp.zeros_like(cx_ref)
        m_ref[...] = jnp.zeros_like(m_ref)

    gts = gt_ref[...] + bg_ref[...]
    lf = -_softplus(-gts)
    bcum = _dot_split_rhs(tril_ref[...], lf, 3)
    bcum = pltpu.roll(bcum, shift=128 - heads, axis=1)
    u_t = (gts - bcum).T
    row = lax.broadcasted_iota(jnp.int32, (tl, tl), 0)
    col = lax.broadcasted_iota(jnp.int32, (tl, tl), 1)
    causal = col <= row
    one_col = (lax.broadcasted_iota(jnp.int32, (tl, dv), 1) == 0).astype(F32)

    hs = range(heads)
    bcol = [bcum[:, h:h + 1] for h in hs]
    m_prev = [m_ref[h, 0:1, 0:1] for h in hs]
    cx = [cx_ref[h] for h in hs]
    umat = [jnp.where(causal, u_t[h:h + 1, :], -jnp.inf) for h in hs]
    cm = [jnp.maximum(m_prev[h], jnp.max(umat[h], axis=1, keepdims=True)) for h in hs]
    sqk = [_dot(q_ref[h], kt_ref[h, 0]) for h in hs]
    qc = [_dot(q_ref[h], cx[h].astype(BF16)) for h in hs]
    w = [jnp.exp(umat[h] - cm[h]) * sqk[h] for h in hs]
    g = [jnp.exp(m_prev[h] - cm[h]) for h in hs]
    m_t = [bcol[h] + cm[h] for h in hs]
    num = [_dot(w[h].astype(BF16), v_ref[h]) + g[h] * qc[h][:, :dv] for h in hs]
    den = [jnp.sum(w[h], axis=1, keepdims=True) + g[h] * qc[h][:, dv:dv + 1] for h in hs]
    hout = [num[h] * (1.0 / jnp.maximum(jnp.abs(den[h]), jnp.exp(-m_t[h]))) for h in hs]
    ms = [jnp.mean(hout[h] * hout[h], axis=1, keepdims=True) for h in hs]
    for h in hs:
        hn = hout[h] * lax.rsqrt(ms[h] + RMS_EPS) * hg_ref[h]
        o_ref[:, h * dv:(h + 1) * dv] = (_sigmoid(og_ref[:, h * dv:(h + 1) * dv]) * hn).astype(BF16)
    for h in hs:
        m_new = m_t[h][tl - 1:tl, :]
        b_end = bcol[h][tl - 1:tl, :]
        wk = jnp.exp(b_end - bcol[h] + gts[:, h:h + 1] - m_new)
        g_end = jnp.exp(b_end + m_prev[h] - m_new)
        vx = (jnp.concatenate([v_ref[h].astype(F32), one_col], axis=1) * wk).astype(BF16)
        cx_ref[h] = g_end * cx[h] + _dot(kt_ref[h, 0], vx)
        m_ref[h] = jnp.broadcast_to(m_new, m_ref.shape[1:])

    @pl.when(c == pl.num_programs(1) - 1)
    def _():
        cx_out_ref[0] = cx_ref[...]
        m_out_ref[0] = m_ref[...]


def _ml_chunk(qb, ktb, vb, og, gates, b_gates_row, head_g, tril, batch):
    heads, m, dqk = qb.shape
    dv = vb.shape[2]
    tl = ktb.shape[3]
    nc = m // batch // tl
    ng = gates.shape[1]
    kern = functools.partial(_ml_chunk_kernel, heads=heads, dv=dv)
    return pl.pallas_call(
        kern, grid=(batch, nc),
        in_specs=[pl.BlockSpec((heads, tl, dqk), lambda b, c: (0, b * nc + c, 0)),
                  pl.BlockSpec((heads, 1, dqk, tl), lambda b, c: (0, b * nc + c, 0, 0)),
                  pl.BlockSpec((heads, tl, dv), lambda b, c: (0, b * nc + c, 0)),
                  pl.BlockSpec((tl, heads * dv), lambda b, c: (b * nc + c, 0)),
                  pl.BlockSpec((tl, ng), lambda b, c: (b * nc + c, 0)),
                  pl.BlockSpec((1, ng), lambda b, c: (0, 0)),
                  pl.BlockSpec((heads, 1, dv), lambda b, c: (0, 0, 0)),
                  pl.BlockSpec((tl, tl), lambda b, c: (0, 0))],
        out_specs=[pl.BlockSpec((tl, heads * dv), lambda b, c: (b * nc + c, 0)),
                   pl.BlockSpec((1, heads, dqk, 2 * dv), lambda b, c: (b, 0, 0, 0)),
                   pl.BlockSpec((1, heads, 8, 128), lambda b, c: (b, 0, 0, 0))],
        out_shape=[jax.ShapeDtypeStruct((m, heads * dv), BF16),
                   jax.ShapeDtypeStruct((batch, heads, dqk, 2 * dv), F32),
                   jax.ShapeDtypeStruct((batch, heads, 8, 128), F32)],
        scratch_shapes=[pltpu.VMEM((heads, dqk, 2 * dv), F32), pltpu.VMEM((heads, 8, 128), F32)],
        compiler_params=_cparams("arbitrary", "arbitrary"), name="ml_chunk",
    )(qb, ktb, vb, og, gates, b_gates_row, head_g, tril)


def _ml_step_kernel(pr_ref, c0_ref, n0_ref, m0_ref, bg_ref, hg_ref,
                    o_ref, c_ref, n_ref, m_ref, *, heads, dqk, dv):
    hq, hv = heads * dqk, heads * dv
    gts = pr_ref[0, :, 2 * hq + 2 * hv:] + bg_ref[...]
    ig = gts[:, 0:heads]
    lf = -_softplus(-gts[:, heads:2 * heads])
    inter = lf + m0_ref[0]
    m_t = jnp.maximum(inter, ig)
    m_ref[0] = m_t
    wgt = jnp.exp(ig - m_t)
    g = jnp.exp(inter - m_t)
    floor = jnp.exp(-m_t)
    eye = (lax.broadcasted_iota(jnp.int32, (dqk, dqk), 0)
           == lax.broadcasted_iota(jnp.int32, (dqk, dqk), 1))

    hs = range(heads)
    q = [pr_ref[0, :, h * dqk:(h + 1) * dqk] * (dqk ** -0.5) for h in hs]
    k = [pr_ref[0, :, hq + h * dqk:hq + (h + 1) * dqk] for h in hs]
    v = [pr_ref[0, :, 2 * hq + h * dv:2 * hq + (h + 1) * dv] for h in hs]
    n0 = [n0_ref[0, h:h + 1, :] for h in hs]
    qk = [jnp.sum(q[h] * k[h], axis=1, keepdims=True) for h in hs]
    qn = [jnp.sum(q[h] * n0[h], axis=1, keepdims=True) for h in hs]
    qc = [_dot(q[h].astype(BF16), c0_ref[0, h].astype(BF16)) for h in hs]
    kv = [_dot(jnp.where(eye, wgt[:, h:h + 1] * k[h], 0.0).astype(BF16),
               jnp.broadcast_to(v[h], (dqk, dv)).astype(BF16)) for h in hs]
    hout = []
    for h in hs:
        w_h, g_h = wgt[:, h:h + 1], g[:, h:h + 1]
        num = (w_h * qk[h]) * v[h] + g_h * qc[h]
        den = w_h * qk[h] + g_h * qn[h]
        hout.append(num * (1.0 / jnp.maximum(jnp.abs(den), floor[:, h:h + 1])))
        c_ref[0, h] = g_h * c0_ref[0, h] + kv[h]
        n_ref[0, h:h + 1, :] = g_h * n0[h] + w_h * k[h]
    ms = [jnp.mean(hout[h] * hout[h], axis=1, keepdims=True) for h in hs]
    for h in hs:
        og = pr_ref[0, :, 2 * hq + hv + h * dv:2 * hq + hv + (h + 1) * dv]
        hn = hout[h] * lax.rsqrt(ms[h] + RMS_EPS) * hg_ref[h]
        o_ref[0, :, h * dv:(h + 1) * dv] = _sigmoid(og) * hn


def _ml_step(proj, c0, n0, m0, b_gates_row, head_g, heads, dqk, dv):
    b, n = proj.shape
    hv = heads * dv
    kern = functools.partial(_ml_step_kernel, heads=heads, dqk=dqk, dv=dv)
    out, c, nn, mm = pl.pallas_call(
        kern, grid=(b,),
        in_specs=[pl.BlockSpec((1, 1, n), lambda i: (i, 0, 0)),
                  pl.BlockSpec((1, heads, dqk, dv), lambda i: (i, 0, 0, 0)),
                  pl.BlockSpec((1, heads, dqk), lambda i: (i, 0, 0)),
                  pl.BlockSpec((1, 1, heads), lambda i: (i, 0, 0)),
                  pl.BlockSpec((1, b_gates_row.shape[1]), lambda i: (0, 0)),
                  pl.BlockSpec((heads, 1, dv), lambda i: (0, 0, 0))],
        out_specs=[pl.BlockSpec((1, 1, hv), lambda i: (i, 0, 0)),
                   pl.BlockSpec((1, heads, dqk, dv), lambda i: (i, 0, 0, 0)),
                   pl.BlockSpec((1, heads, dqk), lambda i: (i, 0, 0)),
                   pl.BlockSpec((1, 1, heads), lambda i: (i, 0, 0))],
        out_shape=[jax.ShapeDtypeStruct((b, 1, hv), F32),
                   jax.ShapeDtypeStruct((b, heads, dqk, dv), F32),
                   jax.ShapeDtypeStruct((b, heads, dqk), F32),
                   jax.ShapeDtypeStruct((b, 1, heads), F32)],
        compiler_params=_cparams("arbitrary"), name="ml_step",
    )(proj.reshape(b, 1, n), c0, n0, m0.reshape(b, 1, heads), b_gates_row, head_g)
    return out.reshape(b, hv), c, nn, mm.reshape(b, heads)


def _suffix_ones(n):
    i = lax.broadcasted_iota(jnp.int32, (n, n), 0)
    j = lax.broadcasted_iota(jnp.int32, (n, n), 1)
    return (i >= j).astype(BF16)


def kernel(x_prompt, x_sample, cache_k, cache_v, state_C, state_n, state_m, page_table, p_prompt, p_sample, norm_g, ffn_w_in, ffn_w_out, sb_w_qkv, sb_w_o, sb_logit_bias, ml_w_in, ml_b_gates, ml_head_g, ml_w_out, ple_w_proj, ple_w_gate, final_norm_g):
    batch, seq, d = x_prompt.shape
    dec_batch = x_sample.shape[0]
    depth = norm_g.shape[0]
    sb_heads = sb_logit_bias.shape[1]
    hd = d // sb_heads
    ml_heads, dv = ml_head_g.shape[1], ml_head_g.shape[2]
    dqk = (ml_w_in.shape[2] - 2 * ml_heads * dv - 2 * ml_heads) // (2 * ml_heads)
    page = cache_k.shape[2]
    mp = batch * seq

    xp = x_prompt.reshape(mp, d)
    xs = x_sample.reshape(dec_batch, d)
    pp = p_prompt.reshape(depth, mp, -1)
    ps = p_sample.reshape(depth, dec_batch, -1)
    pool_kt = jnp.transpose(cache_k, (0, 1, 3, 4, 2))
    pool_vt = jnp.transpose(cache_v, (0, 1, 3, 4, 2))

    sb_tile = min(SB_TILE, seq)
    ml_tile = min(ML_TILE, seq)
    tri_sb = _suffix_ones(sb_tile)
    tril_ml = _suffix_ones(ml_tile)
    tri_dec = _suffix_ones(page)

    gate_pad = 128 - 2 * ml_heads
    norm_rows = norm_g.reshape(depth, 4, 1, d)
    final_row = final_norm_g.reshape(1, d)

    w_ffn_in, w_ffn_out = ffn_w_in.astype(BF16), ffn_w_out.astype(BF16)
    w_sb_qkv, w_sb_o = sb_w_qkv.astype(BF16), sb_w_o.astype(BF16)
    w_ml_in = jnp.pad(ml_w_in, ((0, 0), (0, 0), (0, gate_pad))).astype(BF16)
    w_ml_o = ml_w_out.astype(BF16)
    w_ple_gate, w_ple_proj = ple_w_gate.astype(BF16), ple_w_proj.astype(BF16)

    kv_prompt = None
    ks_l, vs_l = [], []
    cp_l, np_l, mp_l, cs_l, ns_l, ms_l = [], [], [], [], [], []
    for i in range(depth):
        g = norm_rows[i]
        j = i // 2
        xp = _ffn(xp, g[0], _layer(w_ffn_in, i, 0), _layer(w_ffn_out, i, 0))
        xs = _ffn(xs, g[0], _layer(w_ffn_in, i, 0), _layer(w_ffn_out, i, 0))
        if i % 2 == 0:
            w_qkv, w_o = _layer(w_sb_qkv, j), _layer(w_sb_o, j)
            kf, vf, qb, ktb, vb = _sb_proj(xp, g[1], w_qkv, sb_heads, sb_tile, batch, kv_prompt)
            kv_prompt = (kf, vf)
            op = _sb_attn(qb, ktb, vb, sb_logit_bias[j], tri_sb, batch)
            qkv_s = _rms_matmul(xs, g[1], w_qkv)
            ks_l.append(qkv_s[:, d:2 * d].reshape(dec_batch, 1, sb_heads, hd))
            vs_l.append(qkv_s[:, 2 * d:].reshape(dec_batch, 1, sb_heads, hd))
            bias_rep = jnp.broadcast_to(sb_logit_bias[j][:, None], (sb_heads, page))
            os_ = _sb_decode(qkv_s[:, :d], pool_kt, pool_vt, j, page_table, bias_rep,
                             tri_dec, hd ** -0.5).reshape(dec_batch, d)
        else:
            w_in, w_o = _layer(w_ml_in, j), _layer(w_ml_o, j)
            bg_row = jnp.pad(ml_b_gates[j], (0, gate_pad)).reshape(1, 128)
            hg = ml_head_g[j].reshape(ml_heads, 1, dv)
            og, gates, qb, ktb, vb = _ml_proj(xp, g[1], w_in, ml_heads, dqk, dv, ml_tile)
            op, cx, mm = _ml_chunk(qb, ktb, vb, og, gates, bg_row, hg, tril_ml, batch)
            cp_l.append(cx[..., :dv])
            np_l.append(cx[..., dv])
            mp_l.append(mm[:, :, 0, 0])
            proj_s = _rms_matmul(xs, g[1], w_in)
            os_, c_s, n_s, m_s = _ml_step(proj_s, state_C[j], state_n[j], state_m[j], bg_row, hg,
                                          ml_heads, dqk, dv)
            cs_l.append(c_s)
            ns_l.append(n_s)
            ms_l.append(m_s)
        g_final = final_row if i == depth - 1 else None
        post = (g[2], g[3], w_o, _layer(w_ffn_in, i, 1), _layer(w_ffn_out, i, 1),
                _layer(w_ple_gate, i), _layer(w_ple_proj, i), g_final)
        xp = _post_mixer(op, xp, pp[i], *post)
        xs = _post_mixer(os_, xs, ps[i], *post)
    kp, vp = (jnp.transpose(a, (0, 1, 4, 2, 3)) for a in kv_prompt)
    return (xp.reshape(batch, seq, d), xs.reshape(dec_batch, 1, d),
            kp, vp, jnp.stack(cp_l), jnp.stack(np_l), jnp.stack(mp_l),
            jnp.stack(ks_l), jnp.stack(vs_l), jnp.stack(cs_l), jnp.stack(ns_l), jnp.stack(ms_l))
```

```python
import functools

import jax
import jax.numpy as jnp
from jax import lax
from jax.experimental import pallas as pl
from jax.experimental.pallas import tpu as pltpu

F32 = jnp.float32
BF16 = jnp.bfloat16

RMS_EPS = 1e-6
LOG2E = 1.4426950408889634
SOFTPLUS_CLAMP = 100.0
MXU_COLS = 256
VMEM_LIMIT_BYTES = 56 * 1024 * 1024
ROW_TILE = 512
SB_TILE = 256
SB_HEADS_PER_STEP = 8
ML_TILE = 256
PAGES_PER_STEP = 16


def _cparams(*sem):
    return pltpu.CompilerParams(dimension_semantics=sem, vmem_limit_bytes=VMEM_LIMIT_BYTES)


def _const_spec(shape):
    nd = len(shape)
    return pl.BlockSpec(shape, lambda *_: (0,) * nd, pipeline_mode=pl.Buffered(1))


def _layer(w, *idx):
    return (w, idx)


def _wshape(wl):
    w, idx = wl
    return w.shape[len(idx):]


def _wspec(wl):
    w, idx = wl
    shape = w.shape[len(idx):]
    return pl.BlockSpec((None,) * len(idx) + shape, lambda *_: idx + (0,) * len(shape),
                        pipeline_mode=pl.Buffered(1))


def _dot(a, b):
    return jnp.dot(a, b, preferred_element_type=F32)


def _rms(x, g):
    return x * lax.rsqrt(jnp.mean(x * x, axis=-1, keepdims=True) + RMS_EPS) * g


def _sigmoid(x):
    return 1.0 / (1.0 + jnp.exp(-x))


def _softplus(z):
    return jnp.maximum(z, 0.0) + jnp.log(1.0 + jnp.exp2(jnp.abs(z) * (-LOG2E)))


def _softplus_log2(z2):
    return jnp.maximum(jnp.log(1.0 + jnp.exp2(jnp.minimum(z2, SOFTPLUS_CLAMP))) * LOG2E, z2)


def _split_bf16(a, parts):
    out = []
    for _ in range(parts - 1):
        hi = a.astype(BF16)
        out.append(hi)
        a = a - hi.astype(F32)
    out.append(a.astype(BF16))
    return out


def _dot_split_lhs(a, b01, parts):
    acc = None
    for p in _split_bf16(a, parts):
        y = _dot(p, b01)
        acc = y if acc is None else acc + y
    return acc


def _dot_split_rhs(a01, b, parts):
    acc = None
    for p in _split_bf16(b, parts):
        y = _dot(a01, p)
        acc = y if acc is None else acc + y
    return acc


def _col_chunks(n, max_cols):
    assert n % MXU_COLS == 0
    step = max(MXU_COLS, (max_cols // MXU_COLS) * MXU_COLS)
    return [(c, min(c + step, n)) for c in range(0, n, step)]


def _swiglu(h, win_ref, wout_ref, d_ff, chunks):
    acc = None
    for c0, c1 in chunks:
        gate = _dot(h, win_ref[:, c0:c1])
        up = _dot(h, win_ref[:, d_ff + c0:d_ff + c1])
        act = (gate * _sigmoid(gate) * up).astype(BF16)
        y = _dot(act, wout_ref[c0:c1, :])
        acc = y if acc is None else acc + y
    return acc


def _ffn_kernel(x_ref, g_ref, win_ref, wout_ref, o_ref, *, d_ff, chunks):
    x = x_ref[...]
    o_ref[...] = x + 0.5 * _swiglu(_rms(x, g_ref[...]).astype(BF16), win_ref, wout_ref, d_ff, chunks)


def _ffn(x, g, w_in, w_out):
    m, d = x.shape
    d_ff = _wshape(w_out)[0]
    tm = min(ROW_TILE, m)
    kern = functools.partial(_ffn_kernel, d_ff=d_ff, chunks=_col_chunks(d_ff, 1536))
    return pl.pallas_call(
        kern, grid=(m // tm,),
        in_specs=[pl.BlockSpec((tm, d), lambda i: (i, 0)),
                  _const_spec((1, d)), _wspec(w_in), _wspec(w_out)],
        out_specs=pl.BlockSpec((tm, d), lambda i: (i, 0)),
        out_shape=jax.ShapeDtypeStruct((m, d), F32),
        compiler_params=_cparams("arbitrary"), name="ffn",
    )(x, g, w_in[0], w_out[0])


def _rms_matmul_kernel(x_ref, g_ref, w_ref, o_ref):
    h = _rms(x_ref[...], g_ref[...]).astype(BF16)
    o_ref[...] = _dot(h, w_ref[...])


def _rms_matmul(x, g, w):
    m, d = x.shape
    n = _wshape(w)[1]
    return pl.pallas_call(
        _rms_matmul_kernel, grid=(1,),
        in_specs=[_const_spec((m, d)), _const_spec((1, d)), _wspec(w)],
        out_specs=pl.BlockSpec((m, n), lambda i: (0, 0)),
        out_shape=jax.ShapeDtypeStruct((m, n), F32),
        compiler_params=_cparams("arbitrary"), name="rms_matmul",
    )(x, g, w[0])


def _head_major_stores(q, kt, v, qb_ref, ktb_ref, vb_ref, *, heads, dqk, dv, tk):
    tm = q.shape[0]
    for h in range(heads):
        qb_ref[h] = q[:, h * dqk:(h + 1) * dqk].astype(BF16)
        vb_ref[h] = v[:, h * dv:(h + 1) * dv].astype(BF16)
        for r in range(tm // tk):
            ktb_ref[h, r] = kt[h * dqk:(h + 1) * dqk, r * tk:(r + 1) * tk].astype(BF16)


def _sb_proj_kernel(x_ref, g_ref, w_ref, *refs, heads, hd, tk, n_prev):
    if n_prev:
        kprev_ref, vprev_ref = refs[:2]
        refs = refs[2:]
    kf_ref, vf_ref, qb_ref, ktb_ref, vb_ref = refs
    d = heads * hd
    h = _rms(x_ref[...], g_ref[...]).astype(BF16)
    q = _dot(h, w_ref[:, 0:d]) * (hd ** -0.5 * LOG2E)
    k = _dot(h, w_ref[:, d:2 * d])
    v = _dot(h, w_ref[:, 2 * d:3 * d])
    kt = k.T
    if n_prev:
        kf_ref[0:n_prev] = kprev_ref[...]
        vf_ref[0:n_prev] = vprev_ref[...]
    kf_ref[n_prev, 0] = kt.reshape(heads, hd, kt.shape[1])
    vf_ref[n_prev, 0] = v.T.reshape(heads, hd, kt.shape[1])
    _head_major_stores(q, kt, v, qb_ref, ktb_ref, vb_ref, heads=heads, dqk=hd, dv=hd, tk=tk)


def _sb_proj(x, g, w, heads, tk, batch, prev=None):
    m, d = x.shape
    hd = d // heads
    seq = m // batch
    tm = min(ROW_TILE, seq)
    spb = seq // tm
    n_prev = 0 if prev is None else prev[0].shape[0]
    kv_spec = lambda n: pl.BlockSpec((n, 1, heads, hd, tm), lambda i: (0, i // spb, 0, 0, i % spb))
    kv_shape = jax.ShapeDtypeStruct((n_prev + 1, batch, heads, hd, seq), F32)
    kern = functools.partial(_sb_proj_kernel, heads=heads, hd=hd, tk=tk, n_prev=n_prev)
    return pl.pallas_call(
        kern, grid=(m // tm,),
        in_specs=[pl.BlockSpec((tm, d), lambda i: (i, 0)), _const_spec((1, d)), _wspec(w)]
                 + ([kv_spec(n_prev)] * 2 if n_prev else []),
        out_specs=[kv_spec(n_prev + 1), kv_spec(n_prev + 1),
                   pl.BlockSpec((heads, tm, hd), lambda i: (0, i, 0)),
                   pl.BlockSpec((heads, tm // tk, hd, tk), lambda i: (0, i, 0, 0)),
                   pl.BlockSpec((heads, tm, hd), lambda i: (0, i, 0))],
        out_shape=[kv_shape, kv_shape,
                   jax.ShapeDtypeStruct((heads, m, hd), BF16),
                   jax.ShapeDtypeStruct((heads, m // tk, hd, tk), BF16),
                   jax.ShapeDtypeStruct((heads, m, hd), BF16)],
        compiler_params=_cparams("arbitrary"), name="sb_proj",
    )(x, g, w[0], *(prev or ()))


def _ml_proj_kernel(x_ref, g_ref, w_ref, og_ref, gt_ref, qb_ref, ktb_ref, vb_ref,
                    *, heads, dqk, dv, tk):
    hq, hv = heads * dqk, heads * dv
    h = _rms(x_ref[...], g_ref[...]).astype(BF16)
    q = _dot(h, w_ref[:, 0:hq]) * (dqk ** -0.5)
    k = _dot(h, w_ref[:, hq:2 * hq])
    v = _dot(h, w_ref[:, 2 * hq:2 * hq + hv])
    og_ref[...] = _dot(h, w_ref[:, 2 * hq + hv:2 * hq + 2 * hv])
    gt_ref[...] = _dot(h, w_ref[:, 2 * hq + 2 * hv:])
    _head_major_stores(q, k.T, v, qb_ref, ktb_ref, vb_ref, heads=heads, dqk=dqk, dv=dv, tk=tk)


def _ml_proj(x, g, w, heads, dqk, dv, tk):
    m, d = x.shape
    hv = heads * dv
    ng = _wshape(w)[1] - 2 * heads * dqk - 2 * hv
    tm = min(ROW_TILE, m)
    kern = functools.partial(_ml_proj_kernel, heads=heads, dqk=dqk, dv=dv, tk=tk)
    return pl.pallas_call(
        kern, grid=(m // tm,),
        in_specs=[pl.BlockSpec((tm, d), lambda i: (i, 0)), _const_spec((1, d)), _wspec(w)],
        out_specs=[pl.BlockSpec((tm, hv), lambda i: (i, 0)),
                   pl.BlockSpec((tm, ng), lambda i: (i, 0)),
                   pl.BlockSpec((heads, tm, dqk), lambda i: (0, i, 0)),
                   pl.BlockSpec((heads, tm // tk, dqk, tk), lambda i: (0, i, 0, 0)),
                   pl.BlockSpec((heads, tm, dv), lambda i: (0, i, 0))],
        out_shape=[jax.ShapeDtypeStruct((m, hv), F32), jax.ShapeDtypeStruct((m, ng), F32),
                   jax.ShapeDtypeStruct((heads, m, dqk), BF16),
                   jax.ShapeDtypeStruct((heads, m // tk, dqk, tk), BF16),
                   jax.ShapeDtypeStruct((heads, m, dv), BF16)],
        compiler_params=_cparams("arbitrary"), name="ml_proj",
    )(x, g, w[0])


def _post_mixer_kernel(a_ref, x_ref, p_ref, g2_ref, g3_ref, wo_ref, win_ref, wout_ref, wg_ref, wp_ref,
                       *refs, d_ff, chunks, final):
    x = x_ref[...] + _dot(a_ref[...].astype(BF16), wo_ref[...])
    x = x + 0.5 * _swiglu(_rms(x, g2_ref[...]).astype(BF16), win_ref, wout_ref, d_ff, chunks)
    gate = _sigmoid(_dot(_rms(x, g3_ref[...]).astype(BF16), wg_ref[...]))
    out = x + gate * _dot(p_ref[...].astype(BF16), wp_ref[...])
    if final:
        gf_ref, y_ref = refs
        y_ref[...] = _rms(out, gf_ref[...])
    else:
        refs[0][...] = out


def _post_mixer(a, x, p, g2, g3, w_o, w_in, w_out, w_gate, w_proj, g_final=None):
    m, d = x.shape
    p_all, layer = p
    ka, dp, d_ff = a.shape[1], p_all.shape[2], _wshape(w_out)[0]
    tm = min(ROW_TILE, m)
    row = lambda n: pl.BlockSpec((tm, n), lambda i: (i, 0))
    final = g_final is not None
    extra_specs, extra_args = ([_const_spec((1, d))], [g_final]) if final else ([], [])
    kern = functools.partial(_post_mixer_kernel, d_ff=d_ff, chunks=_col_chunks(d_ff, 1536), final=final)
    return pl.pallas_call(
        kern, grid=(m // tm,),
        in_specs=[row(ka), row(d), pl.BlockSpec((None, tm, dp), lambda i: (layer, i, 0)),
                  _const_spec((1, d)), _const_spec((1, d)), _wspec(w_o), _wspec(w_in), _wspec(w_out), _wspec(w_gate), _wspec(w_proj)] + extra_specs,
        out_specs=row(d),
        out_shape=jax.ShapeDtypeStruct((m, d), F32),
        compiler_params=_cparams("arbitrary"), name="post_mixer",
    )(a, x, p_all, g2, g3, w_o[0], w_in[0], w_out[0], w_gate[0], w_proj[0], *extra_args)


MASKED_LOGIT = -1e30


NEXT_SLOT = 2


def _sb_attn_kernel(bias_ref, q_ref, kt_ref, v_ref, tri_ref, o_ref,
                    rsum_ref, acc_ref, z_ref, hi_ref, lo_ref, *, tq, heads_per_step):
    hp = pl.program_id(1)
    i = pl.program_id(2)
    has_next = i + 1 < pl.num_programs(2)
    tri = tri_ref[...]

    all_heads = range(heads_per_step)

    def scores(qi, t, slot, masked=False, heads=all_heads):
        q0 = pl.multiple_of(qi * tq, tq)
        for hh in heads:
            z = (_dot(q_ref[hh, pl.ds(q0, tq), :], kt_ref[hh, qi - t])
                 + bias_ref[hp * heads_per_step + hh] * LOG2E)
            sp = _softplus_log2(z)
            if masked:
                row = lax.broadcasted_iota(jnp.int32, (tq, tq), 0)
                col = lax.broadcasted_iota(jnp.int32, (tq, tq), 1)
                sp = jnp.where(col < row, sp, 0.0)
                z = jnp.where(col < row, z, MASKED_LOGIT)
            hi, lo = _split_bf16(sp, 2)
            z_ref[slot, hh] = z
            hi_ref[slot, hh] = hi
            lo_ref[slot, hh] = lo

    def weights(t, slot, heads=all_heads):
        start = pl.multiple_of((i - t) * tq, tq)
        for hh in heads:
            inc_hi = _dot(hi_ref[slot, hh], tri)
            inc_lo = _dot(lo_ref[slot, hh], tri)
            rsum = rsum_ref[hh]
            base = z_ref[slot, hh] - jnp.concatenate([rsum] * (tq // 128), axis=1)
            a = jnp.exp2((base - inc_hi) - inc_lo)
            acc_ref[hh] += _dot(a.astype(BF16), v_ref[hh, pl.ds(start, tq), :])
            rsum_ref[hh] = rsum + jnp.broadcast_to(inc_hi[:, 0:1] + inc_lo[:, 0:1], rsum.shape)

    def last_weights(slot):
        @pl.when(has_next)
        def _():
            for hh in all_heads:
                weights(i, slot, [hh])
                scores(i + 1, 0, NEXT_SLOT, masked=True, heads=[hh])

        @pl.when(jnp.logical_not(has_next))
        def _():
            weights(i, slot)

    rsum_ref[...] = jnp.zeros_like(rsum_ref)
    acc_ref[...] = jnp.zeros_like(acc_ref)

    @pl.when(i == 0)
    def _():
        scores(i, 0, 0, masked=True)

    @pl.when(i > 0)
    def _():
        z_ref[0] = z_ref[NEXT_SLOT]
        hi_ref[0] = hi_ref[NEXT_SLOT]
        lo_ref[0] = lo_ref[NEXT_SLOT]

    def pair(p, carry):
        t = 2 * p + 1
        for hh in all_heads:
            weights(t - 1, 0, [hh])
            scores(i, t, 1, heads=[hh])
        for hh in all_heads:
            weights(t, 1, [hh])
            scores(i, t + 1, 0, heads=[hh])
        return carry

    lax.fori_loop(0, i // 2, pair, 0)

    @pl.when(i % 2 == 1)
    def _():
        for hh in all_heads:
            weights(i - 1, 0, [hh])
            scores(i, i, 1, heads=[hh])
        last_weights(1)

    @pl.when(i % 2 == 0)
    def _():
        last_weights(0)

    o_ref[...] = jnp.concatenate([acc_ref[hh] for hh in range(heads_per_step)], axis=1).astype(BF16)


def _sb_attn(qb, ktb, vb, bias, tri, batch):
    heads, m, hd = qb.shape
    seq = m // batch
    tq = ktb.shape[3]
    nq = seq // tq
    hps = SB_HEADS_PER_STEP
    kern = functools.partial(_sb_attn_kernel, tq=tq, heads_per_step=hps)
    return pl.pallas_call(
        kern, grid=(batch, heads // hps, nq),
        in_specs=[pl.BlockSpec(memory_space=pltpu.SMEM),
                  pl.BlockSpec((hps, seq, hd), lambda b, h, i: (h, b, 0)),
                  pl.BlockSpec((hps, nq, hd, tq), lambda b, h, i: (h, b, 0, 0)),
                  pl.BlockSpec((hps, seq, hd), lambda b, h, i: (h, b, 0)),
                  _const_spec(tri.shape)],
        out_specs=pl.BlockSpec((tq, hps * hd), lambda b, h, i: (b * nq + i, h)),
        out_shape=jax.ShapeDtypeStruct((m, heads * hd), BF16),
        scratch_shapes=[pltpu.VMEM((hps, tq, 128), F32), pltpu.VMEM((hps, tq, hd), F32),
                        pltpu.VMEM((3, hps, tq, tq), F32), pltpu.VMEM((3, hps, tq, tq), BF16),
                        pltpu.VMEM((3, hps, tq, tq), BF16)],
        compiler_params=_cparams("arbitrary", "arbitrary", "arbitrary"), name="sb_attn",
    )(bias, qb, ktb, vb, tri)


def _sb_decode_kernel(pt_ref, q_ref, bias_ref, tri_ref, *refs, pages, scale):
    del pt_ref
    k_refs, v_refs = refs[:pages], refs[pages:2 * pages]
    o_ref, qrep_ref, rsum_ref, acc_ref = refs[2 * pages:]
    s = pl.program_id(1)
    heads, hd, page = k_refs[0].shape

    @pl.when(s == 0)
    def _():
        qrow = q_ref[0] * scale
        qrep_ref[...] = jnp.broadcast_to(qrow, (page, heads * hd)).T.reshape(heads, hd, page)
        rsum_ref[...] = jnp.zeros_like(rsum_ref)
        acc_ref[...] = jnp.zeros_like(acc_ref)

    qrep = qrep_ref[...]
    bias = bias_ref[...]
    z = jnp.concatenate([jnp.sum(k_refs[i][...] * qrep, axis=1) + bias for i in range(pages)], axis=0)
    sp = _softplus(z)
    inc = _dot_split_lhs(sp, tri_ref[...], 2)
    rsum = rsum_ref[...]
    for i in reversed(range(pages)):
        rows = slice(i * heads, (i + 1) * heads)
        a = jnp.exp(z[rows] - inc[rows] - rsum)
        rsum = rsum + jnp.broadcast_to(inc[rows, 0:1], rsum.shape)
        for h in range(heads):
            acc_ref[h] += v_refs[i][h] * a[h:h + 1, :]
    rsum_ref[...] = rsum

    @pl.when(s == pl.num_programs(1) - 1)
    def _():
        o_ref[0] = jnp.sum(acc_ref[...], axis=2)


def _sb_decode(q, pool_kt, pool_vt, layer, page_table, bias_rep, tri, scale):
    b, d = q.shape
    n_pages = page_table.shape[1]
    heads, hd, page = pool_kt.shape[2:]
    pages = min(PAGES_PER_STEP, n_pages)
    steps = n_pages // pages

    def page_spec(i):
        return pl.BlockSpec((None, None, heads, hd, page),
                            lambda bb, s, pt: (layer, pt[bb, (steps - 1 - s) * pages + i], 0, 0, 0))

    const = lambda shape: pl.BlockSpec(shape, lambda bb, s, pt: (0,) * len(shape))
    kern = functools.partial(_sb_decode_kernel, pages=pages, scale=scale)
    return pl.pallas_call(
        kern,
        grid_spec=pltpu.PrefetchScalarGridSpec(
            num_scalar_prefetch=1, grid=(b, steps),
            in_specs=[pl.BlockSpec((1, 1, d), lambda bb, s, pt: (bb, 0, 0)),
                      const(bias_rep.shape), const(tri.shape)]
                     + [page_spec(i) for i in range(pages)] * 2,
            out_specs=pl.BlockSpec((1, heads, hd), lambda bb, s, pt: (bb, 0, 0)),
            scratch_shapes=[pltpu.VMEM((heads, hd, page), F32), pltpu.VMEM((heads, page), F32),
                            pltpu.VMEM((heads, hd, page), F32)]),
        out_shape=jax.ShapeDtypeStruct((b, heads, hd), F32),
        compiler_params=_cparams("arbitrary", "arbitrary"), name="sb_decode",
    )(page_table, q.reshape(b, 1, d), bias_rep, tri, *([pool_kt] * pages), *([pool_vt] * pages))


def _ml_chunk_kernel(q_ref, kt_ref, v_ref, og_ref, gt_ref, bg_ref, hg_ref, tril_ref,
                     o_ref, cx_out_ref, m_out_ref, cx_ref, m_ref, *, heads, dv):
    c = pl.program_id(1)
    tl = gt_ref.shape[0]

    @pl.when(c == 0)
    def _():
        cx_ref[...] = jnp.zeros_like(cx_ref)
        m_ref[...] = jnp.zeros_like(m_ref)

    gts = gt_ref[...] + bg_ref[...]
    lf = -_softplus(-gts)
    bcum = _dot_split_rhs(tril_ref[...], lf, 3)
    bcum = pltpu.roll(bcum, shift=128 - heads, axis=1)
    u_t = (gts - bcum).T
    row = lax.broadcasted_iota(jnp.int32, (tl, tl), 0)
    col = lax.broadcasted_iota(jnp.int32, (tl, tl), 1)
    causal = col <= row
    one_col = (lax.broadcasted_iota(jnp.int32, (tl, dv), 1) == 0).astype(F32)

    hs = range(heads)
    bcol = [bcum[:, h:h + 1] for h in hs]
    m_prev = [m_ref[h, 0:1, 0:1] for h in hs]
    cx = [cx_ref[h] for h in hs]
    umat = [jnp.where(causal, u_t[h:h + 1, :], -jnp.inf) for h in hs]
    cm = [jnp.maximum(m_prev[h], jnp.max(umat[h], axis=1, keepdims=True)) for h in hs]
    sqk = [_dot(q_ref[h], kt_ref[h, 0]) for h in hs]
    qc = [_dot(q_ref[h], cx[h].astype(BF16)) for h in hs]
    w = [jnp.exp(umat[h] - cm[h]) * sqk[h] for h in hs]
    g = [jnp.exp(m_prev[h] - cm[h]) for h in hs]
    m_t = [bcol[h] + cm[h] for h in hs]
    num = [_dot(w[h].astype(BF16), v_ref[h]) + g[h] * qc[h][:, :dv] for h in hs]
    den = [jnp.sum(w[h], axis=1, keepdims=True) + g[h] * qc[h][:, dv:dv + 1] for h in hs]
    hout = [num[h] * (1.0 / jnp.maximum(jnp.abs(den[h]), jnp.exp(-m_t[h]))) for h in hs]
    ms = [jnp.mean(hout[h] * hout[h], axis=1, keepdims=True) for h in hs]
    for h in hs:
        hn = hout[h] * lax.rsqrt(ms[h] + RMS_EPS) * hg_ref[h]
        o_ref[:, h * dv:(h + 1) * dv] = (_sigmoid(og_ref[:, h * dv:(h + 1) * dv]) * hn).astype(BF16)
    for h in hs:
        m_new = m_t[h][tl - 1:tl, :]
        b_end = bcol[h][tl - 1:tl, :]
        wk = jnp.exp(b_end - bcol[h] + gts[:, h:h + 1] - m_new)
        g_end = jnp.exp(b_end + m_prev[h] - m_new)
        vx = (jnp.concatenate([v_ref[h].astype(F32), one_col], axis=1) * wk).astype(BF16)
        cx_ref[h] = g_end * cx[h] + _dot(kt_ref[h, 0], vx)
        m_ref[h] = jnp.broadcast_to(m_new, m_ref.shape[1:])

    @pl.when(c == pl.num_programs(1) - 1)
    def _():
        cx_out_ref[0] = cx_ref[...]
        m_out_ref[0] = m_ref[...]


def _ml_chunk(qb, ktb, vb, og, gates, b_gates_row, head_g, tril, batch):
    heads, m, dqk = qb.shape
    dv = vb.shape[2]
    tl = ktb.shape[3]
    nc = m // batch // tl
    ng = gates.shape[1]
    kern = functools.partial(_ml_chunk_kernel, heads=heads, dv=dv)
    return pl.pallas_call(
        kern, grid=(batch, nc),
        in_specs=[pl.BlockSpec((heads, tl, dqk), lambda b, c: (0, b * nc + c, 0)),
                  pl.BlockSpec((heads, 1, dqk, tl), lambda b, c: (0, b * nc + c, 0, 0)),
                  pl.BlockSpec((heads, tl, dv), lambda b, c: (0, b * nc + c, 0)),
                  pl.BlockSpec((tl, heads * dv), lambda b, c: (b * nc + c, 0)),
                  pl.BlockSpec((tl, ng), lambda b, c: (b * nc + c, 0)),
                  pl.BlockSpec((1, ng), lambda b, c: (0, 0)),
                  pl.BlockSpec((heads, 1, dv), lambda b, c: (0, 0, 0)),
                  pl.BlockSpec((tl, tl), lambda b, c: (0, 0))],
        out_specs=[pl.BlockSpec((tl, heads * dv), lambda b, c: (b * nc + c, 0)),
                   pl.BlockSpec((1, heads, dqk, 2 * dv), lambda b, c: (b, 0, 0, 0)),
                   pl.BlockSpec((1, heads, 8, 128), lambda b, c: (b, 0, 0, 0))],
        out_shape=[jax.ShapeDtypeStruct((m, heads * dv), BF16),
                   jax.ShapeDtypeStruct((batch, heads, dqk, 2 * dv), F32),
                   jax.ShapeDtypeStruct((batch, heads, 8, 128), F32)],
        scratch_shapes=[pltpu.VMEM((heads, dqk, 2 * dv), F32), pltpu.VMEM((heads, 8, 128), F32)],
        compiler_params=_cparams("arbitrary", "arbitrary"), name="ml_chunk",
    )(qb, ktb, vb, og, gates, b_gates_row, head_g, tril)


def _ml_step_kernel(pr_ref, c0_ref, n0_ref, m0_ref, bg_ref, hg_ref,
                    o_ref, c_ref, n_ref, m_ref, *, heads, dqk, dv):
    hq, hv = heads * dqk, heads * dv
    gts = pr_ref[0, :, 2 * hq + 2 * hv:] + bg_ref[...]
    ig = gts[:, 0:heads]
    lf = -_softplus(-gts[:, heads:2 * heads])
    inter = lf + m0_ref[0]
    m_t = jnp.maximum(inter, ig)
    m_ref[0] = m_t
    wgt = jnp.exp(ig - m_t)
    g = jnp.exp(inter - m_t)
    floor = jnp.exp(-m_t)
    eye = (lax.broadcasted_iota(jnp.int32, (dqk, dqk), 0)
           == lax.broadcasted_iota(jnp.int32, (dqk, dqk), 1))

    hs = range(heads)
    q = [pr_ref[0, :, h * dqk:(h + 1) * dqk] * (dqk ** -0.5) for h in hs]
    k = [pr_ref[0, :, hq + h * dqk:hq + (h + 1) * dqk] for h in hs]
    v = [pr_ref[0, :, 2 * hq + h * dv:2 * hq + (h + 1) * dv] for h in hs]
    n0 = [n0_ref[0, h:h + 1, :] for h in hs]
    qk = [jnp.sum(q[h] * k[h], axis=1, keepdims=True) for h in hs]
    qn = [jnp.sum(q[h] * n0[h], axis=1, keepdims=True) for h in hs]
    qc = [_dot(q[h].astype(BF16), c0_ref[0, h].astype(BF16)) for h in hs]
    kv = [_dot(jnp.where(eye, wgt[:, h:h + 1] * k[h], 0.0).astype(BF16),
               jnp.broadcast_to(v[h], (dqk, dv)).astype(BF16)) for h in hs]
    hout = []
    for h in hs:
        w_h, g_h = wgt[:, h:h + 1], g[:, h:h + 1]
        num = (w_h * qk[h]) * v[h] + g_h * qc[h]
        den = w_h * qk[h] + g_h * qn[h]
        hout.append(num * (1.0 / jnp.maximum(jnp.abs(den), floor[:, h:h + 1])))
        c_ref[0, h] = g_h * c0_ref[0, h] + kv[h]
        n_ref[0, h:h + 1, :] = g_h * n0[h] + w_h * k[h]
    ms = [jnp.mean(hout[h] * hout[h], axis=1, keepdims=True) for h in hs]
    for h in hs:
        og = pr_ref[0, :, 2 * hq + hv + h * dv:2 * hq + hv + (h + 1) * dv]
        hn = hout[h] * lax.rsqrt(ms[h] + RMS_EPS) * hg_ref[h]
        o_ref[0, :, h * dv:(h + 1) * dv] = _sigmoid(og) * hn


def _ml_step(proj, c0, n0, m0, b_gates_row, head_g, heads, dqk, dv):
    b, n = proj.shape
    hv = heads * dv
    kern = functools.partial(_ml_step_kernel, heads=heads, dqk=dqk, dv=dv)
    out, c, nn, mm = pl.pallas_call(
        kern, grid=(b,),
        in_specs=[pl.BlockSpec((1, 1, n), lambda i: (i, 0, 0)),
                  pl.BlockSpec((1, heads, dqk, dv), lambda i: (i, 0, 0, 0)),
                  pl.BlockSpec((1, heads, dqk), lambda i: (i, 0, 0)),
                  pl.BlockSpec((1, 1, heads), lambda i: (i, 0, 0)),
                  pl.BlockSpec((1, b_gates_row.shape[1]), lambda i: (0, 0)),
                  pl.BlockSpec((heads, 1, dv), lambda i: (0, 0, 0))],
        out_specs=[pl.BlockSpec((1, 1, hv), lambda i: (i, 0, 0)),
                   pl.BlockSpec((1, heads, dqk, dv), lambda i: (i, 0, 0, 0)),
                   pl.BlockSpec((1, heads, dqk), lambda i: (i, 0, 0)),
                   pl.BlockSpec((1, 1, heads), lambda i: (i, 0, 0))],
        out_shape=[jax.ShapeDtypeStruct((b, 1, hv), F32),
                   jax.ShapeDtypeStruct((b, heads, dqk, dv), F32),
                   jax.ShapeDtypeStruct((b, heads, dqk), F32),
                   jax.ShapeDtypeStruct((b, 1, heads), F32)],
        compiler_params=_cparams("arbitrary"), name="ml_step",
    )(proj.reshape(b, 1, n), c0, n0, m0.reshape(b, 1, heads), b_gates_row, head_g)
    return out.reshape(b, hv), c, nn, mm.reshape(b, heads)


def _suffix_ones(n):
    i = lax.broadcasted_iota(jnp.int32, (n, n), 0)
    j = lax.broadcasted_iota(jnp.int32, (n, n), 1)
    return (i >= j).astype(BF16)


def kernel(x_prompt, x_sample, cache_k, cache_v, state_C, state_n, state_m, page_table, p_prompt, p_sample, norm_g, ffn_w_in, ffn_w_out, sb_w_qkv, sb_w_o, sb_logit_bias, ml_w_in, ml_b_gates, ml_head_g, ml_w_out, ple_w_proj, ple_w_gate, final_norm_g):
    batch, seq, d = x_prompt.shape
    dec_batch = x_sample.shape[0]
    depth = norm_g.shape[0]
    sb_heads = sb_logit_bias.shape[1]
    hd = d // sb_heads
    ml_heads, dv = ml_head_g.shape[1], ml_head_g.shape[2]
    dqk = (ml_w_in.shape[2] - 2 * ml_heads * dv - 2 * ml_heads) // (2 * ml_heads)
    page = cache_k.shape[2]
    mp = batch * seq

    xp = x_prompt.reshape(mp, d)
    xs = x_sample.reshape(dec_batch, d)
    pp = p_prompt.reshape(depth, mp, -1)
    ps = p_sample.reshape(depth, dec_batch, -1)
    pool_kt = jnp.transpose(cache_k, (0, 1, 3, 4, 2))
    pool_vt = jnp.transpose(cache_v, (0, 1, 3, 4, 2))

    sb_tile = min(SB_TILE, seq)
    ml_tile = min(ML_TILE, seq)
    tri_sb = _suffix_ones(sb_tile)
    tril_ml = _suffix_ones(ml_tile)
    tri_dec = _suffix_ones(page)

    gate_pad = 128 - 2 * ml_heads
    norm_rows = norm_g.reshape(depth, 4, 1, d)
    final_row = final_norm_g.reshape(1, d)

    w_ffn_in, w_ffn_out = ffn_w_in.astype(BF16), ffn_w_out.astype(BF16)
    w_sb_qkv, w_sb_o = sb_w_qkv.astype(BF16), sb_w_o.astype(BF16)
    w_ml_in = jnp.pad(ml_w_in, ((0, 0), (0, 0), (0, gate_pad))).astype(BF16)
    w_ml_o = ml_w_out.astype(BF16)
    w_ple_gate, w_ple_proj = ple_w_gate.astype(BF16), ple_w_proj.astype(BF16)

    kv_prompt = None
    ks_l, vs_l = [], []
    cp_l, np_l, mp_l, cs_l, ns_l, ms_l = [], [], [], [], [], []
    for i in range(depth):
        g = norm_rows[i]
        j = i // 2
        xp = _ffn(xp, g[0], _layer(w_ffn_in, i, 0), _layer(w_ffn_out, i, 0))
        xs = _ffn(xs, g[0], _layer(w_ffn_in, i, 0), _layer(w_ffn_out, i, 0))
        if i % 2 == 0:
            w_qkv, w_o = _layer(w_sb_qkv, j), _layer(w_sb_o, j)
            kf, vf, qb, ktb, vb = _sb_proj(xp, g[1], w_qkv, sb_heads, sb_tile, batch, kv_prompt)
            kv_prompt = (kf, vf)
            op = _sb_attn(qb, ktb, vb, sb_logit_bias[j], tri_sb, batch)
            qkv_s = _rms_matmul(xs, g[1], w_qkv)
            ks_l.append(qkv_s[:, d:2 * d].reshape(dec_batch, 1, sb_heads, hd))
            vs_l.append(qkv_s[:, 2 * d:].reshape(dec_batch, 1, sb_heads, hd))
            bias_rep = jnp.broadcast_to(sb_logit_bias[j][:, None], (sb_heads, page))
            os_ = _sb_decode(qkv_s[:, :d], pool_kt, pool_vt, j, page_table, bias_rep,
                             tri_dec, hd ** -0.5).reshape(dec_batch, d)
        else:
            w_in, w_o = _layer(w_ml_in, j), _layer(w_ml_o, j)
            bg_row = jnp.pad(ml_b_gates[j], (0, gate_pad)).reshape(1, 128)
            hg = ml_head_g[j].reshape(ml_heads, 1, dv)
            og, gates, qb, ktb, vb = _ml_proj(xp, g[1], w_in, ml_heads, dqk, dv, ml_tile)
            op, cx, mm = _ml_chunk(qb, ktb, vb, og, gates, bg_row, hg, tril_ml, batch)
            cp_l.append(cx[..., :dv])
            np_l.append(cx[..., dv])
            mp_l.append(mm[:, :, 0, 0])
            proj_s = _rms_matmul(xs, g[1], w_in)
            os_, c_s, n_s, m_s = _ml_step(proj_s, state_C[j], state_n[j], state_m[j], bg_row, hg,
                                          ml_heads, dqk, dv)
            cs_l.append(c_s)
            ns_l.append(n_s)
            ms_l.append(m_s)
        g_final = final_row if i == depth - 1 else None
        post = (g[2], g[3], w_o, _layer(w_ffn_in, i, 1), _layer(w_ffn_out, i, 1),
                _layer(w_ple_gate, i), _layer(w_ple_proj, i), g_final)
        xp = _post_mixer(op, xp, (pp, i), *post)
        xs = _post_mixer(os_, xs, (ps, i), *post)
    kp, vp = (jnp.transpose(a, (0, 1, 4, 2, 3)) for a in kv_prompt)
    return (xp.reshape(batch, seq, d), xs.reshape(dec_batch, 1, d),
            kp, vp, jnp.stack(cp_l), jnp.stack(np_l), jnp.stack(mp_l),
            jnp.stack(ks_l), jnp.stack(vs_l), jnp.stack(cs_l), jnp.stack(ns_l), jnp.stack(ms_l))
```

```python
import functools

import jax
import jax.numpy as jnp
from jax import lax
from jax.experimental import pallas as pl
from jax.experimental.pallas import tpu as pltpu

F32 = jnp.float32
BF16 = jnp.bfloat16

RMS_EPS = 1e-6
LOG2E = 1.4426950408889634
SOFTPLUS_CLAMP = 100.0
MXU_COLS = 256
VMEM_LIMIT_BYTES = 56 * 1024 * 1024
ROW_TILE = 512
SB_TILE = 256
SB_HEADS_PER_STEP = 8
ML_TILE = 256
PAGES_PER_STEP = 16


def _cparams(*sem):
    return pltpu.CompilerParams(dimension_semantics=sem, vmem_limit_bytes=VMEM_LIMIT_BYTES)


def _const_spec(shape):
    nd = len(shape)
    return pl.BlockSpec(shape, lambda *_: (0,) * nd, pipeline_mode=pl.Buffered(1))


def _layer(w, *idx):
    return (w, idx)


def _wshape(wl):
    w, idx = wl
    return w.shape[len(idx):]


def _wspec(wl):
    w, idx = wl
    shape = w.shape[len(idx):]
    return pl.BlockSpec((None,) * len(idx) + shape, lambda *_: idx + (0,) * len(shape),
                        pipeline_mode=pl.Buffered(1))


def _dot(a, b):
    return jnp.dot(a, b, preferred_element_type=F32)


def _rms(x, g):
    return x * lax.rsqrt(jnp.mean(x * x, axis=-1, keepdims=True) + RMS_EPS) * g


def _sigmoid(x):
    return 1.0 / (1.0 + jnp.exp(-x))


def _softplus(z):
    return jnp.maximum(z, 0.0) + jnp.log(1.0 + jnp.exp2(jnp.abs(z) * (-LOG2E)))


def _softplus_log2(z2):
    return jnp.maximum(jnp.log(1.0 + jnp.exp2(jnp.minimum(z2, SOFTPLUS_CLAMP))) * LOG2E, z2)


def _split_bf16(a, parts):
    out = []
    for _ in range(parts - 1):
        hi = a.astype(BF16)
        out.append(hi)
        a = a - hi.astype(F32)
    out.append(a.astype(BF16))
    return out


def _dot_split_lhs(a, b01, parts):
    acc = None
    for p in _split_bf16(a, parts):
        y = _dot(p, b01)
        acc = y if acc is None else acc + y
    return acc


def _dot_split_rhs(a01, b, parts):
    acc = None
    for p in _split_bf16(b, parts):
        y = _dot(a01, p)
        acc = y if acc is None else acc + y
    return acc


def _col_chunks(n, max_cols):
    assert n % MXU_COLS == 0
    step = max(MXU_COLS, (max_cols // MXU_COLS) * MXU_COLS)
    return [(c, min(c + step, n)) for c in range(0, n, step)]


def _swiglu(h, win_ref, wout_ref, d_ff, chunks):
    acc = None
    for c0, c1 in chunks:
        gate = _dot(h, win_ref[:, c0:c1])
        up = _dot(h, win_ref[:, d_ff + c0:d_ff + c1])
        act = (gate * _sigmoid(gate) * up).astype(BF16)
        y = _dot(act, wout_ref[c0:c1, :])
        acc = y if acc is None else acc + y
    return acc


def _ffn_kernel(x_ref, g_ref, win_ref, wout_ref, o_ref, *, d_ff, chunks):
    x = x_ref[...]
    o_ref[...] = x + 0.5 * _swiglu(_rms(x, g_ref[...]).astype(BF16), win_ref, wout_ref, d_ff, chunks)


def _ffn(x, g, w_in, w_out):
    m, d = x.shape
    d_ff = _wshape(w_out)[0]
    tm = min(ROW_TILE, m)
    kern = functools.partial(_ffn_kernel, d_ff=d_ff, chunks=_col_chunks(d_ff, 1536))
    return pl.pallas_call(
        kern, grid=(m // tm,),
        in_specs=[pl.BlockSpec((tm, d), lambda i: (i, 0)),
                  _const_spec((1, d)), _wspec(w_in), _wspec(w_out)],
        out_specs=pl.BlockSpec((tm, d), lambda i: (i, 0)),
        out_shape=jax.ShapeDtypeStruct((m, d), F32),
        compiler_params=_cparams("arbitrary"), name="ffn",
    )(x, g, w_in[0], w_out[0])


def _rms_matmul_kernel(x_ref, g_ref, w_ref, o_ref):
    h = _rms(x_ref[...], g_ref[...]).astype(BF16)
    o_ref[...] = _dot(h, w_ref[...])


def _rms_matmul(x, g, w):
    m, d = x.shape
    n = _wshape(w)[1]
    return pl.pallas_call(
        _rms_matmul_kernel, grid=(1,),
        in_specs=[_const_spec((m, d)), _const_spec((1, d)), _wspec(w)],
        out_specs=pl.BlockSpec((m, n), lambda i: (0, 0)),
        out_shape=jax.ShapeDtypeStruct((m, n), F32),
        compiler_params=_cparams("arbitrary"), name="rms_matmul",
    )(x, g, w[0])


def _head_major_stores(q, kt, v, qb_ref, ktb_ref, vb_ref, *, heads, dqk, dv, tk):
    tm = q.shape[0]
    for h in range(heads):
        qb_ref[h] = q[:, h * dqk:(h + 1) * dqk].astype(BF16)
        vb_ref[h] = v[:, h * dv:(h + 1) * dv].astype(BF16)
        for r in range(tm // tk):
            ktb_ref[h, r] = kt[h * dqk:(h + 1) * dqk, r * tk:(r + 1) * tk].astype(BF16)


def _sb_proj_kernel(x_ref, g_ref, w_ref, *refs, heads, hd, tk, n_prev):
    if n_prev:
        kprev_ref, vprev_ref = refs[:2]
        refs = refs[2:]
    kf_ref, vf_ref, qb_ref, ktb_ref, vb_ref = refs
    d = heads * hd
    h = _rms(x_ref[...], g_ref[...]).astype(BF16)
    q = _dot(h, w_ref[:, 0:d]) * (hd ** -0.5 * LOG2E)
    k = _dot(h, w_ref[:, d:2 * d])
    v = _dot(h, w_ref[:, 2 * d:3 * d])
    kt = k.T
    if n_prev:
        kf_ref[0:n_prev] = kprev_ref[...]
        vf_ref[0:n_prev] = vprev_ref[...]
    kf_ref[n_prev, 0] = kt.reshape(heads, hd, kt.shape[1])
    vf_ref[n_prev, 0] = v.T.reshape(heads, hd, kt.shape[1])
    _head_major_stores(q, kt, v, qb_ref, ktb_ref, vb_ref, heads=heads, dqk=hd, dv=hd, tk=tk)


def _sb_proj(x, g, w, heads, tk, batch, prev=None):
    m, d = x.shape
    hd = d // heads
    seq = m // batch
    tm = min(ROW_TILE, seq)
    spb = seq // tm
    n_prev = 0 if prev is None else prev[0].shape[0]
    kv_spec = lambda n: pl.BlockSpec((n, 1, heads, hd, tm), lambda i: (0, i // spb, 0, 0, i % spb))
    kv_shape = jax.ShapeDtypeStruct((n_prev + 1, batch, heads, hd, seq), F32)
    kern = functools.partial(_sb_proj_kernel, heads=heads, hd=hd, tk=tk, n_prev=n_prev)
    return pl.pallas_call(
        kern, grid=(m // tm,),
        in_specs=[pl.BlockSpec((tm, d), lambda i: (i, 0)), _const_spec((1, d)), _wspec(w)]
                 + ([kv_spec(n_prev)] * 2 if n_prev else []),
        out_specs=[kv_spec(n_prev + 1), kv_spec(n_prev + 1),
                   pl.BlockSpec((heads, tm, hd), lambda i: (0, i, 0)),
                   pl.BlockSpec((heads, tm // tk, hd, tk), lambda i: (0, i, 0, 0)),
                   pl.BlockSpec((heads, tm, hd), lambda i: (0, i, 0))],
        out_shape=[kv_shape, kv_shape,
                   jax.ShapeDtypeStruct((heads, m, hd), BF16),
                   jax.ShapeDtypeStruct((heads, m // tk, hd, tk), BF16),
                   jax.ShapeDtypeStruct((heads, m, hd), BF16)],
        compiler_params=_cparams("arbitrary"), name="sb_proj",
    )(x, g, w[0], *(prev or ()))


def _ml_proj_kernel(x_ref, g_ref, w_ref, og_ref, gt_ref, qb_ref, ktb_ref, vb_ref,
                    *, heads, dqk, dv, tk):
    hq, hv = heads * dqk, heads * dv
    h = _rms(x_ref[...], g_ref[...]).astype(BF16)
    q = _dot(h, w_ref[:, 0:hq]) * (dqk ** -0.5)
    k = _dot(h, w_ref[:, hq:2 * hq])
    v = _dot(h, w_ref[:, 2 * hq:2 * hq + hv])
    og_ref[...] = _dot(h, w_ref[:, 2 * hq + hv:2 * hq + 2 * hv])
    gt_ref[...] = _dot(h, w_ref[:, 2 * hq + 2 * hv:])
    _head_major_stores(q, k.T, v, qb_ref, ktb_ref, vb_ref, heads=heads, dqk=dqk, dv=dv, tk=tk)


def _ml_proj(x, g, w, heads, dqk, dv, tk):
    m, d = x.shape
    hv = heads * dv
    ng = _wshape(w)[1] - 2 * heads * dqk - 2 * hv
    tm = min(ROW_TILE, m)
    kern = functools.partial(_ml_proj_kernel, heads=heads, dqk=dqk, dv=dv, tk=tk)
    return pl.pallas_call(
        kern, grid=(m // tm,),
        in_specs=[pl.BlockSpec((tm, d), lambda i: (i, 0)), _const_spec((1, d)), _wspec(w)],
        out_specs=[pl.BlockSpec((tm, hv), lambda i: (i, 0)),
                   pl.BlockSpec((tm, ng), lambda i: (i, 0)),
                   pl.BlockSpec((heads, tm, dqk), lambda i: (0, i, 0)),
                   pl.BlockSpec((heads, tm // tk, dqk, tk), lambda i: (0, i, 0, 0)),
                   pl.BlockSpec((heads, tm, dv), lambda i: (0, i, 0))],
        out_shape=[jax.ShapeDtypeStruct((m, hv), F32), jax.ShapeDtypeStruct((m, ng), F32),
                   jax.ShapeDtypeStruct((heads, m, dqk), BF16),
                   jax.ShapeDtypeStruct((heads, m // tk, dqk, tk), BF16),
                   jax.ShapeDtypeStruct((heads, m, dv), BF16)],
        compiler_params=_cparams("arbitrary"), name="ml_proj",
    )(x, g, w[0])


def _post_mixer_kernel(a_ref, x_ref, p_ref, g2_ref, g3_ref, wo_ref, win_ref, wout_ref, wg_ref, wp_ref,
                       *refs, d_ff, chunks, final):
    x = x_ref[...] + _dot(a_ref[...].astype(BF16), wo_ref[...])
    x = x + 0.5 * _swiglu(_rms(x, g2_ref[...]).astype(BF16), win_ref, wout_ref, d_ff, chunks)
    gate = _sigmoid(_dot(_rms(x, g3_ref[...]).astype(BF16), wg_ref[...]))
    out = x + gate * _dot(p_ref[...].astype(BF16), wp_ref[...])
    if final:
        gf_ref, y_ref = refs
        y_ref[...] = _rms(out, gf_ref[...])
    else:
        refs[0][...] = out


def _post_mixer(a, x, p, g2, g3, w_o, w_in, w_out, w_gate, w_proj, g_final=None):
    m, d = x.shape
    p_all, layer = p
    ka, dp, d_ff = a.shape[1], p_all.shape[2], _wshape(w_out)[0]
    tm = min(ROW_TILE, m)
    row = lambda n: pl.BlockSpec((tm, n), lambda i: (i, 0))
    final = g_final is not None
    extra_specs, extra_args = ([_const_spec((1, d))], [g_final]) if final else ([], [])
    kern = functools.partial(_post_mixer_kernel, d_ff=d_ff, chunks=_col_chunks(d_ff, 1536), final=final)
    return pl.pallas_call(
        kern, grid=(m // tm,),
        in_specs=[row(ka), row(d), pl.BlockSpec((None, tm, dp), lambda i: (layer, i, 0)),
                  _const_spec((1, d)), _const_spec((1, d)), _wspec(w_o), _wspec(w_in), _wspec(w_out), _wspec(w_gate), _wspec(w_proj)] + extra_specs,
        out_specs=row(d),
        out_shape=jax.ShapeDtypeStruct((m, d), F32),
        compiler_params=_cparams("arbitrary"), name="post_mixer",
    )(a, x, p_all, g2, g3, w_o[0], w_in[0], w_out[0], w_gate[0], w_proj[0], *extra_args)


MASKED_LOGIT = -1e30


NEXT_SLOT = 2


def _sb_attn_kernel(bias_ref, q_ref, kt_ref, v_ref, tri_ref, o_ref,
                    rsum_ref, acc_ref, z_ref, hi_ref, lo_ref, a_ref, *, tq, heads_per_step):
    hp = pl.program_id(1)
    i = pl.program_id(2)
    has_next = i + 1 < pl.num_programs(2)
    tri = tri_ref[...]

    all_heads = range(heads_per_step)

    def scores(qi, t, slot, masked=False, heads=all_heads):
        q0 = pl.multiple_of(qi * tq, tq)
        for hh in heads:
            z = (_dot(q_ref[hh, pl.ds(q0, tq), :], kt_ref[hh, qi - t])
                 + bias_ref[hp * heads_per_step + hh] * LOG2E)
            sp = _softplus_log2(z)
            if masked:
                row = lax.broadcasted_iota(jnp.int32, (tq, tq), 0)
                col = lax.broadcasted_iota(jnp.int32, (tq, tq), 1)
                sp = jnp.where(col < row, sp, 0.0)
                z = jnp.where(col < row, z, MASKED_LOGIT)
            hi, lo = _split_bf16(sp, 2)
            z_ref[slot, hh] = z
            hi_ref[slot, hh] = hi
            lo_ref[slot, hh] = lo

    def weights(t, slot, heads=all_heads):
        vstart = pl.multiple_of(jnp.minimum(i - t + 1, i) * tq, tq)
        for hh in heads:
            acc_ref[hh] += _dot(a_ref[1 - slot, hh], v_ref[hh, pl.ds(vstart, tq), :])
            inc_hi = _dot(hi_ref[slot, hh], tri)
            inc_lo = _dot(lo_ref[slot, hh], tri)
            rsum = rsum_ref[hh]
            base = z_ref[slot, hh] - jnp.concatenate([rsum] * (tq // 128), axis=1)
            a_ref[slot, hh] = jnp.exp2((base - inc_hi) - inc_lo).astype(BF16)
            rsum_ref[hh] = rsum + jnp.broadcast_to(inc_hi[:, 0:1] + inc_lo[:, 0:1], rsum.shape)

    def last_weights(slot):
        @pl.when(has_next)
        def _():
            for hh in all_heads:
                weights(i, slot, [hh])
                scores(i + 1, 0, NEXT_SLOT, masked=True, heads=[hh])

        @pl.when(jnp.logical_not(has_next))
        def _():
            weights(i, slot)

        for hh in all_heads:
            acc_ref[hh] += _dot(a_ref[slot, hh], v_ref[hh, pl.ds(0, tq), :])

    rsum_ref[...] = jnp.zeros_like(rsum_ref)
    acc_ref[...] = jnp.zeros_like(acc_ref)
    a_ref[1] = jnp.zeros_like(a_ref[1])

    @pl.when(i == 0)
    def _():
        scores(i, 0, 0, masked=True)

    @pl.when(i > 0)
    def _():
        z_ref[0] = z_ref[NEXT_SLOT]
        hi_ref[0] = hi_ref[NEXT_SLOT]
        lo_ref[0] = lo_ref[NEXT_SLOT]

    def pair(p, carry):
        t = 2 * p + 1
        for hh in all_heads:
            weights(t - 1, 0, [hh])
            scores(i, t, 1, heads=[hh])
        for hh in all_heads:
            weights(t, 1, [hh])
            scores(i, t + 1, 0, heads=[hh])
        return carry

    lax.fori_loop(0, i // 2, pair, 0)

    @pl.when(i % 2 == 1)
    def _():
        for hh in all_heads:
            weights(i - 1, 0, [hh])
            scores(i, i, 1, heads=[hh])
        last_weights(1)

    @pl.when(i % 2 == 0)
    def _():
        last_weights(0)

    o_ref[...] = jnp.concatenate([acc_ref[hh] for hh in range(heads_per_step)], axis=1).astype(BF16)


def _sb_attn(qb, ktb, vb, bias, tri, batch):
    heads, m, hd = qb.shape
    seq = m // batch
    tq = ktb.shape[3]
    nq = seq // tq
    hps = SB_HEADS_PER_STEP
    kern = functools.partial(_sb_attn_kernel, tq=tq, heads_per_step=hps)
    return pl.pallas_call(
        kern, grid=(batch, heads // hps, nq),
        in_specs=[pl.BlockSpec(memory_space=pltpu.SMEM),
                  pl.BlockSpec((hps, seq, hd), lambda b, h, i: (h, b, 0)),
                  pl.BlockSpec((hps, nq, hd, tq), lambda b, h, i: (h, b, 0, 0)),
                  pl.BlockSpec((hps, seq, hd), lambda b, h, i: (h, b, 0)),
                  _const_spec(tri.shape)],
        out_specs=pl.BlockSpec((tq, hps * hd), lambda b, h, i: (b * nq + i, h)),
        out_shape=jax.ShapeDtypeStruct((m, heads * hd), BF16),
        scratch_shapes=[pltpu.VMEM((hps, tq, 128), F32), pltpu.VMEM((hps, tq, hd), F32),
                        pltpu.VMEM((3, hps, tq, tq), F32), pltpu.VMEM((3, hps, tq, tq), BF16),
                        pltpu.VMEM((3, hps, tq, tq), BF16), pltpu.VMEM((2, hps, tq, tq), BF16)],
        compiler_params=_cparams("arbitrary", "arbitrary", "arbitrary"), name="sb_attn",
    )(bias, qb, ktb, vb, tri)


def _sb_decode_kernel(pt_ref, q_ref, bias_ref, tri_ref, *refs, pages, scale):
    del pt_ref
    k_refs, v_refs = refs[:pages], refs[pages:2 * pages]
    o_ref, qrep_ref, rsum_ref, acc_ref = refs[2 * pages:]
    s = pl.program_id(1)
    heads, hd, page = k_refs[0].shape

    @pl.when(s == 0)
    def _():
        qrow = q_ref[0] * scale
        qrep_ref[...] = jnp.broadcast_to(qrow, (page, heads * hd)).T.reshape(heads, hd, page)
        rsum_ref[...] = jnp.zeros_like(rsum_ref)
        acc_ref[...] = jnp.zeros_like(acc_ref)

    qrep = qrep_ref[...]
    bias = bias_ref[...]
    z = jnp.concatenate([jnp.sum(k_refs[i][...] * qrep, axis=1) + bias for i in range(pages)], axis=0)
    sp = _softplus(z)
    inc = _dot_split_lhs(sp, tri_ref[...], 2)
    rsum = rsum_ref[...]
    for i in reversed(range(pages)):
        rows = slice(i * heads, (i + 1) * heads)
        a = jnp.exp(z[rows] - inc[rows] - rsum)
        rsum = rsum + jnp.broadcast_to(inc[rows, 0:1], rsum.shape)
        for h in range(heads):
            acc_ref[h] += v_refs[i][h] * a[h:h + 1, :]
    rsum_ref[...] = rsum

    @pl.when(s == pl.num_programs(1) - 1)
    def _():
        o_ref[0] = jnp.sum(acc_ref[...], axis=2)


def _sb_decode(q, pool_kt, pool_vt, layer, page_table, bias_rep, tri, scale):
    b, d = q.shape
    n_pages = page_table.shape[1]
    heads, hd, page = pool_kt.shape[2:]
    pages = min(PAGES_PER_STEP, n_pages)
    steps = n_pages // pages

    def page_spec(i):
        return pl.BlockSpec((None, None, heads, hd, page),
                            lambda bb, s, pt: (layer, pt[bb, (steps - 1 - s) * pages + i], 0, 0, 0))

    const = lambda shape: pl.BlockSpec(shape, lambda bb, s, pt: (0,) * len(shape))
    kern = functools.partial(_sb_decode_kernel, pages=pages, scale=scale)
    return pl.pallas_call(
        kern,
        grid_spec=pltpu.PrefetchScalarGridSpec(
            num_scalar_prefetch=1, grid=(b, steps),
            in_specs=[pl.BlockSpec((1, 1, d), lambda bb, s, pt: (bb, 0, 0)),
                      const(bias_rep.shape), const(tri.shape)]
                     + [page_spec(i) for i in range(pages)] * 2,
            out_specs=pl.BlockSpec((1, heads, hd), lambda bb, s, pt: (bb, 0, 0)),
            scratch_shapes=[pltpu.VMEM((heads, hd, page), F32), pltpu.VMEM((heads, page), F32),
                            pltpu.VMEM((heads, hd, page), F32)]),
        out_shape=jax.ShapeDtypeStruct((b, heads, hd), F32),
        compiler_params=_cparams("arbitrary", "arbitrary"), name="sb_decode",
    )(page_table, q.reshape(b, 1, d), bias_rep, tri, *([pool_kt] * pages), *([pool_vt] * pages))


def _ml_chunk_kernel(q_ref, kt_ref, v_ref, og_ref, gt_ref, bg_ref, hg_ref, tril_ref,
                     o_ref, cx_out_ref, m_out_ref, cx_ref, m_ref, *, heads, dv):
    c = pl.program_id(1)
    tl = gt_ref.shape[0]

    @pl.when(c == 0)
    def _():
        cx_ref[...] = jnp.zeros_like(cx_ref)
        m_ref[...] = jnp.zeros_like(m_ref)

    gts = gt_ref[...] + bg_ref[...]
    lf = -_softplus(-gts)
    bcum = _dot_split_rhs(tril_ref[...], lf, 3)
    bcum = pltpu.roll(bcum, shift=128 - heads, axis=1)
    u_t = (gts - bcum).T
    row = lax.broadcasted_iota(jnp.int32, (tl, tl), 0)
    col = lax.broadcasted_iota(jnp.int32, (tl, tl), 1)
    causal = col <= row
    one_col = (lax.broadcasted_iota(jnp.int32, (tl, dv), 1) == 0).astype(F32)

    hs = range(heads)
    bcol = [bcum[:, h:h + 1] for h in hs]
    m_prev = [m_ref[h, 0:1, 0:1] for h in hs]
    cx = [cx_ref[h] for h in hs]
    umat = [jnp.where(causal, u_t[h:h + 1, :], -jnp.inf) for h in hs]
    cm = [jnp.maximum(m_prev[h], jnp.max(umat[h], axis=1, keepdims=True)) for h in hs]
    sqk = [_dot(q_ref[h], kt_ref[h, 0]) for h in hs]
    qc = [_dot(q_ref[h], cx[h].astype(BF16)) for h in hs]
    w = [jnp.exp(umat[h] - cm[h]) * sqk[h] for h in hs]
    g = [jnp.exp(m_prev[h] - cm[h]) for h in hs]
    m_t = [bcol[h] + cm[h] for h in hs]
    num = [_dot(w[h].astype(BF16), v_ref[h]) + g[h] * qc[h][:, :dv] for h in hs]
    den = [jnp.sum(w[h], axis=1, keepdims=True) + g[h] * qc[h][:, dv:dv + 1] for h in hs]
    hout = [num[h] * (1.0 / jnp.maximum(jnp.abs(den[h]), jnp.exp(-m_t[h]))) for h in hs]
    ms = [jnp.mean(hout[h] * hout[h], axis=1, keepdims=True) for h in hs]
    for h in hs:
        hn = hout[h] * lax.rsqrt(ms[h] + RMS_EPS) * hg_ref[h]
        o_ref[:, h * dv:(h + 1) * dv] = (_sigmoid(og_ref[:, h * dv:(h + 1) * dv]) * hn).astype(BF16)
    for h in hs:
        m_new = m_t[h][tl - 1:tl, :]
        b_end = bcol[h][tl - 1:tl, :]
        wk = jnp.exp(b_end - bcol[h] + gts[:, h:h + 1] - m_new)
        g_end = jnp.exp(b_end + m_prev[h] - m_new)
        vx = (jnp.concatenate([v_ref[h].astype(F32), one_col], axis=1) * wk).astype(BF16)
        cx_ref[h] = g_end * cx[h] + _dot(kt_ref[h, 0], vx)
        m_ref[h] = jnp.broadcast_to(m_new, m_ref.shape[1:])

    @pl.when(c == pl.num_programs(1) - 1)
    def _():
        cx_out_ref[0] = cx_ref[...]
        m_out_ref[0] = m_ref[...]


def _ml_chunk(qb, ktb, vb, og, gates, b_gates_row, head_g, tril, batch):
    heads, m, dqk = qb.shape
    dv = vb.shape[2]
    tl = ktb.shape[3]
    nc = m // batch // tl
    ng = gates.shape[1]
    kern = functools.partial(_ml_chunk_kernel, heads=heads, dv=dv)
    return pl.pallas_call(
        kern, grid=(batch, nc),
        in_specs=[pl.BlockSpec((heads, tl, dqk), lambda b, c: (0, b * nc + c, 0)),
                  pl.BlockSpec((heads, 1, dqk, tl), lambda b, c: (0, b * nc + c, 0, 0)),
                  pl.BlockSpec((heads, tl, dv), lambda b, c: (0, b * nc + c, 0)),
                  pl.BlockSpec((tl, heads * dv), lambda b, c: (b * nc + c, 0)),
                  pl.BlockSpec((tl, ng), lambda b, c: (b * nc + c, 0)),
                  pl.BlockSpec((1, ng), lambda b, c: (0, 0)),
                  pl.BlockSpec((heads, 1, dv), lambda b, c: (0, 0, 0)),
                  pl.BlockSpec((tl, tl), lambda b, c: (0, 0))],
        out_specs=[pl.BlockSpec((tl, heads * dv), lambda b, c: (b * nc + c, 0)),
                   pl.BlockSpec((1, heads, dqk, 2 * dv), lambda b, c: (b, 0, 0, 0)),
                   pl.BlockSpec((1, heads, 8, 128), lambda b, c: (b, 0, 0, 0))],
        out_shape=[jax.ShapeDtypeStruct((m, heads * dv), BF16),
                   jax.ShapeDtypeStruct((batch, heads, dqk, 2 * dv), F32),
                   jax.ShapeDtypeStruct((batch, heads, 8, 128), F32)],
        scratch_shapes=[pltpu.VMEM((heads, dqk, 2 * dv), F32), pltpu.VMEM((heads, 8, 128), F32)],
        compiler_params=_cparams("arbitrary", "arbitrary"), name="ml_chunk",
    )(qb, ktb, vb, og, gates, b_gates_row, head_g, tril)


def _ml_step_kernel(pr_ref, c0_ref, n0_ref, m0_ref, bg_ref, hg_ref,
                    o_ref, c_ref, n_ref, m_ref, *, heads, dqk, dv):
    hq, hv = heads * dqk, heads * dv
    gts = pr_ref[0, :, 2 * hq + 2 * hv:] + bg_ref[...]
    ig = gts[:, 0:heads]
    lf = -_softplus(-gts[:, heads:2 * heads])
    inter = lf + m0_ref[0]
    m_t = jnp.maximum(inter, ig)
    m_ref[0] = m_t
    wgt = jnp.exp(ig - m_t)
    g = jnp.exp(inter - m_t)
    floor = jnp.exp(-m_t)
    eye = (lax.broadcasted_iota(jnp.int32, (dqk, dqk), 0)
           == lax.broadcasted_iota(jnp.int32, (dqk, dqk), 1))

    hs = range(heads)
    q = [pr_ref[0, :, h * dqk:(h + 1) * dqk] * (dqk ** -0.5) for h in hs]
    k = [pr_ref[0, :, hq + h * dqk:hq + (h + 1) * dqk] for h in hs]
    v = [pr_ref[0, :, 2 * hq + h * dv:2 * hq + (h + 1) * dv] for h in hs]
    n0 = [n0_ref[0, h:h + 1, :] for h in hs]
    qk = [jnp.sum(q[h] * k[h], axis=1, keepdims=True) for h in hs]
    qn = [jnp.sum(q[h] * n0[h], axis=1, keepdims=True) for h in hs]
    qc = [_dot(q[h].astype(BF16), c0_ref[0, h].astype(BF16)) for h in hs]
    kv = [_dot(jnp.where(eye, wgt[:, h:h + 1] * k[h], 0.0).astype(BF16),
               jnp.broadcast_to(v[h], (dqk, dv)).astype(BF16)) for h in hs]
    hout = []
    for h in hs:
        w_h, g_h = wgt[:, h:h + 1], g[:, h:h + 1]
        num = (w_h * qk[h]) * v[h] + g_h * qc[h]
        den = w_h * qk[h] + g_h * qn[h]
        hout.append(num * (1.0 / jnp.maximum(jnp.abs(den), floor[:, h:h + 1])))
        c_ref[0, h] = g_h * c0_ref[0, h] + kv[h]
        n_ref[0, h:h + 1, :] = g_h * n0[h] + w_h * k[h]
    ms = [jnp.mean(hout[h] * hout[h], axis=1, keepdims=True) for h in hs]
    for h in hs:
        og = pr_ref[0, :, 2 * hq + hv + h * dv:2 * hq + hv + (h + 1) * dv]
        hn = hout[h] * lax.rsqrt(ms[h] + RMS_EPS) * hg_ref[h]
        o_ref[0, :, h * dv:(h + 1) * dv] = _sigmoid(og) * hn


def _ml_step(proj, c0, n0, m0, b_gates_row, head_g, heads, dqk, dv):
    b, n = proj.shape
    hv = heads * dv
    kern = functools.partial(_ml_step_kernel, heads=heads, dqk=dqk, dv=dv)
    out, c, nn, mm = pl.pallas_call(
        kern, grid=(b,),
        in_specs=[pl.BlockSpec((1, 1, n), lambda i: (i, 0, 0)),
                  pl.BlockSpec((1, heads, dqk, dv), lambda i: (i, 0, 0, 0)),
                  pl.BlockSpec((1, heads, dqk), lambda i: (i, 0, 0)),
                  pl.BlockSpec((1, 1, heads), lambda i: (i, 0, 0)),
                  pl.BlockSpec((1, b_gates_row.shape[1]), lambda i: (0, 0)),
                  pl.BlockSpec((heads, 1, dv), lambda i: (0, 0, 0))],
        out_specs=[pl.BlockSpec((1, 1, hv), lambda i: (i, 0, 0)),
                   pl.BlockSpec((1, heads, dqk, dv), lambda i: (i, 0, 0, 0)),
                   pl.BlockSpec((1, heads, dqk), lambda i: (i, 0, 0)),
                   pl.BlockSpec((1, 1, heads), lambda i: (i, 0, 0))],
        out_shape=[jax.ShapeDtypeStruct((b, 1, hv), F32),
                   jax.ShapeDtypeStruct((b, heads, dqk, dv), F32),
                   jax.ShapeDtypeStruct((b, heads, dqk), F32),
                   jax.ShapeDtypeStruct((b, 1, heads), F32)],
        compiler_params=_cparams("arbitrary"), name="ml_step",
    )(proj.reshape(b, 1, n), c0, n0, m0.reshape(b, 1, heads), b_gates_row, head_g)
    return out.reshape(b, hv), c, nn, mm.reshape(b, heads)


def _suffix_ones(n):
    i = lax.broadcasted_iota(jnp.int32, (n, n), 0)
    j = lax.broadcasted_iota(jnp.int32, (n, n), 1)
    return (i >= j).astype(BF16)


def kernel(x_prompt, x_sample, cache_k, cache_v, state_C, state_n, state_m, page_table, p_prompt, p_sample, norm_g, ffn_w_in, ffn_w_out, sb_w_qkv, sb_w_o, sb_logit_bias, ml_w_in, ml_b_gates, ml_head_g, ml_w_out, ple_w_proj, ple_w_gate, final_norm_g):
    batch, seq, d = x_prompt.shape
    dec_batch = x_sample.shape[0]
    depth = norm_g.shape[0]
    sb_heads = sb_logit_bias.shape[1]
    hd = d // sb_heads
    ml_heads, dv = ml_head_g.shape[1], ml_head_g.shape[2]
    dqk = (ml_w_in.shape[2] - 2 * ml_heads * dv - 2 * ml_heads) // (2 * ml_heads)
    page = cache_k.shape[2]
    mp = batch * seq

    xp = x_prompt.reshape(mp, d)
    xs = x_sample.reshape(dec_batch, d)
    pp = p_prompt.reshape(depth, mp, -1)
    ps = p_sample.reshape(depth, dec_batch, -1)
    pool_kt = jnp.transpose(cache_k, (0, 1, 3, 4, 2))
    pool_vt = jnp.transpose(cache_v, (0, 1, 3, 4, 2))

    sb_tile = min(SB_TILE, seq)
    ml_tile = min(ML_TILE, seq)
    tri_sb = _suffix_ones(sb_tile)
    tril_ml = _suffix_ones(ml_tile)
    tri_dec = _suffix_ones(page)

    gate_pad = 128 - 2 * ml_heads
    norm_rows = norm_g.reshape(depth, 4, 1, d)
    final_row = final_norm_g.reshape(1, d)

    w_ffn_in, w_ffn_out = ffn_w_in.astype(BF16), ffn_w_out.astype(BF16)
    w_sb_qkv, w_sb_o = sb_w_qkv.astype(BF16), sb_w_o.astype(BF16)
    w_ml_in = jnp.pad(ml_w_in, ((0, 0), (0, 0), (0, gate_pad))).astype(BF16)
    w_ml_o = ml_w_out.astype(BF16)
    w_ple_gate, w_ple_proj = ple_w_gate.astype(BF16), ple_w_proj.astype(BF16)

    kv_prompt = None
    ks_l, vs_l = [], []
    cp_l, np_l, mp_l, cs_l, ns_l, ms_l = [], [], [], [], [], []
    for i in range(depth):
        g = norm_rows[i]
        j = i // 2
        xp = _ffn(xp, g[0], _layer(w_ffn_in, i, 0), _layer(w_ffn_out, i, 0))
        xs = _ffn(xs, g[0], _layer(w_ffn_in, i, 0), _layer(w_ffn_out, i, 0))
        if i % 2 == 0:
            w_qkv, w_o = _layer(w_sb_qkv, j), _layer(w_sb_o, j)
            kf, vf, qb, ktb, vb = _sb_proj(xp, g[1], w_qkv, sb_heads, sb_tile, batch, kv_prompt)
            kv_prompt = (kf, vf)
            op = _sb_attn(qb, ktb, vb, sb_logit_bias[j], tri_sb, batch)
            qkv_s = _rms_matmul(xs, g[1], w_qkv)
            ks_l.append(qkv_s[:, d:2 * d].reshape(dec_batch, 1, sb_heads, hd))
            vs_l.append(qkv_s[:, 2 * d:].reshape(dec_batch, 1, sb_heads, hd))
            bias_rep = jnp.broadcast_to(sb_logit_bias[j][:, None], (sb_heads, page))
            os_ = _sb_decode(qkv_s[:, :d], pool_kt, pool_vt, j, page_table, bias_rep,
                             tri_dec, hd ** -0.5).reshape(dec_batch, d)
        else:
            w_in, w_o = _layer(w_ml_in, j), _layer(w_ml_o, j)
            bg_row = jnp.pad(ml_b_gates[j], (0, gate_pad)).reshape(1, 128)
            hg = ml_head_g[j].reshape(ml_heads, 1, dv)
            og, gates, qb, ktb, vb = _ml_proj(xp, g[1], w_in, ml_heads, dqk, dv, ml_tile)
            op, cx, mm = _ml_chunk(qb, ktb, vb, og, gates, bg_row, hg, tril_ml, batch)
            cp_l.append(cx[..., :dv])
            np_l.append(cx[..., dv])
            mp_l.append(mm[:, :, 0, 0])
            proj_s = _rms_matmul(xs, g[1], w_in)
            os_, c_s, n_s, m_s = _ml_step(proj_s, state_C[j], state_n[j], state_m[j], bg_row, hg,
                                          ml_heads, dqk, dv)
            cs_l.append(c_s)
            ns_l.append(n_s)
            ms_l.append(m_s)
        g_final = final_row if i == depth - 1 else None
        post = (g[2], g[3], w_o, _layer(w_ffn_in, i, 1), _layer(w_ffn_out, i, 1),
                _layer(w_ple_gate, i), _layer(w_ple_proj, i), g_final)
        xp = _post_mixer(op, xp, (pp, i), *post)
        xs = _post_mixer(os_, xs, (ps, i), *post)
    kp, vp = (jnp.transpose(a, (0, 1, 4, 2, 3)) for a in kv_prompt)
    return (xp.reshape(batch, seq, d), xs.reshape(dec_batch, 1, d),
            kp, vp, jnp.stack(cp_l), jnp.stack(np_l), jnp.stack(mp_l),
            jnp.stack(ks_l), jnp.stack(vs_l), jnp.stack(cs_l), jnp.stack(ns_l), jnp.stack(ms_l))
```

```python
import functools

import jax
import jax.numpy as jnp
from jax import lax
from jax.experimental import pallas as pl
from jax.experimental.pallas import tpu as pltpu

F32 = jnp.float32
BF16 = jnp.bfloat16

RMS_EPS = 1e-6
LOG2E = 1.4426950408889634
SOFTPLUS_CLAMP = 100.0
MXU_COLS = 256
VMEM_LIMIT_BYTES = 56 * 1024 * 1024
ROW_TILE = 512
SB_TILE = 256
SB_HEADS_PER_STEP = 8
ML_TILE = 256
PAGES_PER_STEP = 16


def _cparams(*sem):
    return pltpu.CompilerParams(dimension_semantics=sem, vmem_limit_bytes=VMEM_LIMIT_BYTES)


def _const_spec(shape):
    nd = len(shape)
    return pl.BlockSpec(shape, lambda *_: (0,) * nd, pipeline_mode=pl.Buffered(1))


def _layer(w, *idx):
    return (w, idx)


def _wshape(wl):
    w, idx = wl
    return w.shape[len(idx):]


def _wspec(wl):
    w, idx = wl
    shape = w.shape[len(idx):]
    return pl.BlockSpec((None,) * len(idx) + shape, lambda *_: idx + (0,) * len(shape),
                        pipeline_mode=pl.Buffered(1))


def _dot(a, b):
    return jnp.dot(a, b, preferred_element_type=F32)


def _rms(x, g):
    return x * lax.rsqrt(jnp.mean(x * x, axis=-1, keepdims=True) + RMS_EPS) * g


def _sigmoid(x):
    return 1.0 / (1.0 + jnp.exp(-x))


def _softplus(z):
    return jnp.maximum(z, 0.0) + jnp.log(1.0 + jnp.exp2(jnp.abs(z) * (-LOG2E)))


def _softplus_log2(z2):
    return jnp.maximum(jnp.log(1.0 + jnp.exp2(jnp.minimum(z2, SOFTPLUS_CLAMP))) * LOG2E, z2)


def _split_bf16(a, parts):
    out = []
    for _ in range(parts - 1):
        hi = a.astype(BF16)
        out.append(hi)
        a = a - hi.astype(F32)
    out.append(a.astype(BF16))
    return out


def _dot_split_lhs(a, b01, parts):
    acc = None
    for p in _split_bf16(a, parts):
        y = _dot(p, b01)
        acc = y if acc is None else acc + y
    return acc


def _dot_split_rhs(a01, b, parts):
    acc = None
    for p in _split_bf16(b, parts):
        y = _dot(a01, p)
        acc = y if acc is None else acc + y
    return acc


def _col_chunks(n, max_cols):
    assert n % MXU_COLS == 0
    step = max(MXU_COLS, (max_cols // MXU_COLS) * MXU_COLS)
    return [(c, min(c + step, n)) for c in range(0, n, step)]


def _swiglu(h, win_ref, wout_ref, d_ff, chunks):
    acc = None
    for c0, c1 in chunks:
        gate = _dot(h, win_ref[:, c0:c1])
        up = _dot(h, win_ref[:, d_ff + c0:d_ff + c1])
        act = (gate * _sigmoid(gate) * up).astype(BF16)
        y = _dot(act, wout_ref[c0:c1, :])
        acc = y if acc is None else acc + y
    return acc


def _ffn_kernel(x_ref, g_ref, win_ref, wout_ref, o_ref, *, d_ff, chunks):
    x = x_ref[...]
    o_ref[...] = x + 0.5 * _swiglu(_rms(x, g_ref[...]).astype(BF16), win_ref, wout_ref, d_ff, chunks)


def _ffn(x, g, w_in, w_out):
    m, d = x.shape
    d_ff = _wshape(w_out)[0]
    tm = min(ROW_TILE, m)
    kern = functools.partial(_ffn_kernel, d_ff=d_ff, chunks=_col_chunks(d_ff, 1536))
    return pl.pallas_call(
        kern, grid=(m // tm,),
        in_specs=[pl.BlockSpec((tm, d), lambda i: (i, 0)),
                  _const_spec((1, d)), _wspec(w_in), _wspec(w_out)],
        out_specs=pl.BlockSpec((tm, d), lambda i: (i, 0)),
        out_shape=jax.ShapeDtypeStruct((m, d), F32),
        compiler_params=_cparams("arbitrary"), name="ffn",
    )(x, g, w_in[0], w_out[0])


def _rms_matmul_kernel(x_ref, g_ref, w_ref, o_ref):
    h = _rms(x_ref[...], g_ref[...]).astype(BF16)
    o_ref[...] = _dot(h, w_ref[...])


def _rms_matmul(x, g, w):
    m, d = x.shape
    n = _wshape(w)[1]
    return pl.pallas_call(
        _rms_matmul_kernel, grid=(1,),
        in_specs=[_const_spec((m, d)), _const_spec((1, d)), _wspec(w)],
        out_specs=pl.BlockSpec((m, n), lambda i: (0, 0)),
        out_shape=jax.ShapeDtypeStruct((m, n), F32),
        compiler_params=_cparams("arbitrary"), name="rms_matmul",
    )(x, g, w[0])


def _head_major_stores(q, kt, v, qb_ref, ktb_ref, vb_ref, *, heads, dqk, dv, tk):
    tm = q.shape[0]
    for h in range(heads):
        qb_ref[h] = q[:, h * dqk:(h + 1) * dqk].astype(BF16)
        vb_ref[h] = v[:, h * dv:(h + 1) * dv].astype(BF16)
        for r in range(tm // tk):
            ktb_ref[h, r] = kt[h * dqk:(h + 1) * dqk, r * tk:(r + 1) * tk].astype(BF16)


def _sb_proj_kernel(x_ref, g_ref, w_ref, *refs, heads, hd, tk, n_prev):
    if n_prev:
        kprev_ref, vprev_ref = refs[:2]
        refs = refs[2:]
    kf_ref, vf_ref, qb_ref, ktb_ref, vb_ref = refs
    d = heads * hd
    h = _rms(x_ref[...], g_ref[...]).astype(BF16)
    q = _dot(h, w_ref[:, 0:d]) * (hd ** -0.5 * LOG2E)
    k = _dot(h, w_ref[:, d:2 * d])
    v = _dot(h, w_ref[:, 2 * d:3 * d])
    kt = k.T
    if n_prev:
        kf_ref[0:n_prev] = kprev_ref[...]
        vf_ref[0:n_prev] = vprev_ref[...]
    kf_ref[n_prev, 0] = kt.reshape(heads, hd, kt.shape[1])
    vf_ref[n_prev, 0] = v.T.reshape(heads, hd, kt.shape[1])
    _head_major_stores(q, kt, v, qb_ref, ktb_ref, vb_ref, heads=heads, dqk=hd, dv=hd, tk=tk)


def _sb_proj(x, g, w, heads, tk, batch, prev=None):
    m, d = x.shape
    hd = d // heads
    seq = m // batch
    tm = min(ROW_TILE, seq)
    spb = seq // tm
    n_prev = 0 if prev is None else prev[0].shape[0]
    kv_spec = lambda n: pl.BlockSpec((n, 1, heads, hd, tm), lambda i: (0, i // spb, 0, 0, i % spb))
    kv_shape = jax.ShapeDtypeStruct((n_prev + 1, batch, heads, hd, seq), F32)
    kern = functools.partial(_sb_proj_kernel, heads=heads, hd=hd, tk=tk, n_prev=n_prev)
    return pl.pallas_call(
        kern, grid=(m // tm,),
        in_specs=[pl.BlockSpec((tm, d), lambda i: (i, 0)), _const_spec((1, d)), _wspec(w)]
                 + ([kv_spec(n_prev)] * 2 if n_prev else []),
        out_specs=[kv_spec(n_prev + 1), kv_spec(n_prev + 1),
                   pl.BlockSpec((heads, tm, hd), lambda i: (0, i, 0)),
                   pl.BlockSpec((heads, tm // tk, hd, tk), lambda i: (0, i, 0, 0)),
                   pl.BlockSpec((heads, tm, hd), lambda i: (0, i, 0))],
        out_shape=[kv_shape, kv_shape,
                   jax.ShapeDtypeStruct((heads, m, hd), BF16),
                   jax.ShapeDtypeStruct((heads, m // tk, hd, tk), BF16),
                   jax.ShapeDtypeStruct((heads, m, hd), BF16)],
        compiler_params=_cparams("arbitrary"), name="sb_proj",
    )(x, g, w[0], *(prev or ()))


def _ml_proj_kernel(x_ref, g_ref, w_ref, og_ref, gt_ref, qb_ref, ktb_ref, vb_ref,
                    *, heads, dqk, dv, tk):
    hq, hv = heads * dqk, heads * dv
    h = _rms(x_ref[...], g_ref[...]).astype(BF16)
    q = _dot(h, w_ref[:, 0:hq]) * (dqk ** -0.5)
    k = _dot(h, w_ref[:, hq:2 * hq])
    v = _dot(h, w_ref[:, 2 * hq:2 * hq + hv])
    og_ref[...] = _dot(h, w_ref[:, 2 * hq + hv:2 * hq + 2 * hv])
    gt_ref[...] = _dot(h, w_ref[:, 2 * hq + 2 * hv:])
    _head_major_stores(q, k.T, v, qb_ref, ktb_ref, vb_ref, heads=heads, dqk=dqk, dv=dv, tk=tk)


def _ml_proj(x, g, w, heads, dqk, dv, tk):
    m, d = x.shape
    hv = heads * dv
    ng = _wshape(w)[1] - 2 * heads * dqk - 2 * hv
    tm = min(ROW_TILE, m)
    kern = functools.partial(_ml_proj_kernel, heads=heads, dqk=dqk, dv=dv, tk=tk)
    return pl.pallas_call(
        kern, grid=(m // tm,),
        in_specs=[pl.BlockSpec((tm, d), lambda i: (i, 0)), _const_spec((1, d)), _wspec(w)],
        out_specs=[pl.BlockSpec((tm, hv), lambda i: (i, 0)),
                   pl.BlockSpec((tm, ng), lambda i: (i, 0)),
                   pl.BlockSpec((heads, tm, dqk), lambda i: (0, i, 0)),
                   pl.BlockSpec((heads, tm // tk, dqk, tk), lambda i: (0, i, 0, 0)),
                   pl.BlockSpec((heads, tm, dv), lambda i: (0, i, 0))],
        out_shape=[jax.ShapeDtypeStruct((m, hv), F32), jax.ShapeDtypeStruct((m, ng), F32),
                   jax.ShapeDtypeStruct((heads, m, dqk), BF16),
                   jax.ShapeDtypeStruct((heads, m // tk, dqk, tk), BF16),
                   jax.ShapeDtypeStruct((heads, m, dv), BF16)],
        compiler_params=_cparams("arbitrary"), name="ml_proj",
    )(x, g, w[0])


def _post_mixer_kernel(a_ref, x_ref, p_ref, g2_ref, g3_ref, wo_ref, win_ref, wout_ref, wg_ref, wp_ref,
                       *refs, d_ff, chunks, final):
    x = x_ref[...] + _dot(a_ref[...].astype(BF16), wo_ref[...])
    x = x + 0.5 * _swiglu(_rms(x, g2_ref[...]).astype(BF16), win_ref, wout_ref, d_ff, chunks)
    gate = _sigmoid(_dot(_rms(x, g3_ref[...]).astype(BF16), wg_ref[...]))
    out = x + gate * _dot(p_ref[...].astype(BF16), wp_ref[...])
    if final:
        gf_ref, y_ref = refs
        y_ref[...] = _rms(out, gf_ref[...])
    else:
        refs[0][...] = out


def _post_mixer(a, x, p, g2, g3, w_o, w_in, w_out, w_gate, w_proj, g_final=None):
    m, d = x.shape
    p_all, layer = p
    ka, dp, d_ff = a.shape[1], p_all.shape[2], _wshape(w_out)[0]
    tm = min(ROW_TILE, m)
    row = lambda n: pl.BlockSpec((tm, n), lambda i: (i, 0))
    final = g_final is not None
    extra_specs, extra_args = ([_const_spec((1, d))], [g_final]) if final else ([], [])
    kern = functools.partial(_post_mixer_kernel, d_ff=d_ff, chunks=_col_chunks(d_ff, 1536), final=final)
    return pl.pallas_call(
        kern, grid=(m // tm,),
        in_specs=[row(ka), row(d), pl.BlockSpec((None, tm, dp), lambda i: (layer, i, 0)),
                  _const_spec((1, d)), _const_spec((1, d)), _wspec(w_o), _wspec(w_in), _wspec(w_out), _wspec(w_gate), _wspec(w_proj)] + extra_specs,
        out_specs=row(d),
        out_shape=jax.ShapeDtypeStruct((m, d), F32),
        compiler_params=_cparams("arbitrary"), name="post_mixer",
    )(a, x, p_all, g2, g3, w_o[0], w_in[0], w_out[0], w_gate[0], w_proj[0], *extra_args)


MASKED_LOGIT = -1e30


NEXT_SLOT = 2


def _sb_attn_kernel(bias_ref, q_ref, kt_ref, v_ref, tri_ref, o_ref,
                    rsum_ref, acc_ref, z_ref, hi_ref, lo_ref, a_ref, *, tq, heads_per_step):
    hp = pl.program_id(1)
    i = pl.program_id(2)
    has_next = i + 1 < pl.num_programs(2)
    tri = tri_ref[...]

    all_heads = range(heads_per_step)

    def scores(qi, t, slot, masked=False, heads=all_heads):
        q0 = pl.multiple_of(qi * tq, tq)
        for hh in heads:
            z = (_dot(q_ref[hh, pl.ds(q0, tq), :], kt_ref[hh, qi - t])
                 + bias_ref[hp * heads_per_step + hh] * LOG2E)
            sp = _softplus_log2(z)
            if masked:
                row = lax.broadcasted_iota(jnp.int32, (tq, tq), 0)
                col = lax.broadcasted_iota(jnp.int32, (tq, tq), 1)
                sp = jnp.where(col < row, sp, 0.0)
                z = jnp.where(col < row, z, MASKED_LOGIT)
            hi, lo = _split_bf16(sp, 2)
            z_ref[slot, hh] = z
            hi_ref[slot, hh] = hi
            lo_ref[slot, hh] = lo

    def weights(t, slot, heads=all_heads):
        vstart = pl.multiple_of(jnp.minimum(i - t + 1, i) * tq, tq)
        for hh in heads:
            acc_ref[hh] += _dot(a_ref[1 - slot, hh], v_ref[hh, pl.ds(vstart, tq), :])
            inc_hi = _dot(hi_ref[slot, hh], tri)
            inc_lo = _dot(lo_ref[slot, hh], tri)
            rsum = rsum_ref[hh]
            base = z_ref[slot, hh] - jnp.concatenate([rsum] * (tq // 128), axis=1)
            a_ref[slot, hh] = jnp.exp2((base - inc_hi) - inc_lo).astype(BF16)
            rsum_ref[hh] = rsum + jnp.broadcast_to(inc_hi[:, 0:1] + inc_lo[:, 0:1], rsum.shape)

    def last_weights(slot):
        weights(i, slot)

        @pl.when(has_next)
        def _():
            for hh in all_heads:
                acc_ref[hh] += _dot(a_ref[slot, hh], v_ref[hh, pl.ds(0, tq), :])
                scores(i + 1, 0, NEXT_SLOT, masked=True, heads=[hh])

        @pl.when(jnp.logical_not(has_next))
        def _():
            for hh in all_heads:
                acc_ref[hh] += _dot(a_ref[slot, hh], v_ref[hh, pl.ds(0, tq), :])

    rsum_ref[...] = jnp.zeros_like(rsum_ref)
    acc_ref[...] = jnp.zeros_like(acc_ref)
    a_ref[1] = jnp.zeros_like(a_ref[1])

    @pl.when(i == 0)
    def _():
        scores(i, 0, 0, masked=True)

    @pl.when(i > 0)
    def _():
        z_ref[0] = z_ref[NEXT_SLOT]
        hi_ref[0] = hi_ref[NEXT_SLOT]
        lo_ref[0] = lo_ref[NEXT_SLOT]

    def pair(p, carry):
        t = 2 * p + 1
        for hh in all_heads:
            weights(t - 1, 0, [hh])
            scores(i, t, 1, heads=[hh])
        for hh in all_heads:
            weights(t, 1, [hh])
            scores(i, t + 1, 0, heads=[hh])
        return carry

    lax.fori_loop(0, i // 2, pair, 0)

    @pl.when(i % 2 == 1)
    def _():
        for hh in all_heads:
            weights(i - 1, 0, [hh])
            scores(i, i, 1, heads=[hh])
        last_weights(1)

    @pl.when(i % 2 == 0)
    def _():
        last_weights(0)

    o_ref[...] = jnp.concatenate([acc_ref[hh] for hh in range(heads_per_step)], axis=1).astype(BF16)


def _sb_attn(qb, ktb, vb, bias, tri, batch):
    heads, m, hd = qb.shape
    seq = m // batch
    tq = ktb.shape[3]
    nq = seq // tq
    hps = SB_HEADS_PER_STEP
    kern = functools.partial(_sb_attn_kernel, tq=tq, heads_per_step=hps)
    return pl.pallas_call(
        kern, grid=(batch, heads // hps, nq),
        in_specs=[pl.BlockSpec(memory_space=pltpu.SMEM),
                  pl.BlockSpec((hps, seq, hd), lambda b, h, i: (h, b, 0)),
                  pl.BlockSpec((hps, nq, hd, tq), lambda b, h, i: (h, b, 0, 0)),
                  pl.BlockSpec((hps, seq, hd), lambda b, h, i: (h, b, 0)),
                  _const_spec(tri.shape)],
        out_specs=pl.BlockSpec((tq, hps * hd), lambda b, h, i: (b * nq + i, h)),
        out_shape=jax.ShapeDtypeStruct((m, heads * hd), BF16),
        scratch_shapes=[pltpu.VMEM((hps, tq, 128), F32), pltpu.VMEM((hps, tq, hd), F32),
                        pltpu.VMEM((3, hps, tq, tq), F32), pltpu.VMEM((3, hps, tq, tq), BF16),
                        pltpu.VMEM((3, hps, tq, tq), BF16), pltpu.VMEM((2, hps, tq, tq), BF16)],
        compiler_params=_cparams("arbitrary", "arbitrary", "arbitrary"), name="sb_attn",
    )(bias, qb, ktb, vb, tri)


def _sb_decode_kernel(pt_ref, q_ref, bias_ref, tri_ref, *refs, pages, scale):
    del pt_ref
    k_refs, v_refs = refs[:pages], refs[pages:2 * pages]
    o_ref, qrep_ref, rsum_ref, acc_ref = refs[2 * pages:]
    s = pl.program_id(1)
    heads, hd, page = k_refs[0].shape

    @pl.when(s == 0)
    def _():
        qrow = q_ref[0] * scale
        qrep_ref[...] = jnp.broadcast_to(qrow, (page, heads * hd)).T.reshape(heads, hd, page)
        rsum_ref[...] = jnp.zeros_like(rsum_ref)
        acc_ref[...] = jnp.zeros_like(acc_ref)

    qrep = qrep_ref[...]
    bias = bias_ref[...]
    z = jnp.concatenate([jnp.sum(k_refs[i][...] * qrep, axis=1) + bias for i in range(pages)], axis=0)
    sp = _softplus(z)
    inc = _dot_split_lhs(sp, tri_ref[...], 2)
    rsum = rsum_ref[...]
    for i in reversed(range(pages)):
        rows = slice(i * heads, (i + 1) * heads)
        a = jnp.exp(z[rows] - inc[rows] - rsum)
        rsum = rsum + jnp.broadcast_to(inc[rows, 0:1], rsum.shape)
        for h in range(heads):
            acc_ref[h] += v_refs[i][h] * a[h:h + 1, :]
    rsum_ref[...] = rsum

    @pl.when(s == pl.num_programs(1) - 1)
    def _():
        o_ref[0] = jnp.sum(acc_ref[...], axis=2)


def _sb_decode(q, pool_kt, pool_vt, layer, page_table, bias_rep, tri, scale):
    b, d = q.shape
    n_pages = page_table.shape[1]
    heads, hd, page = pool_kt.shape[2:]
    pages = min(PAGES_PER_STEP, n_pages)
    steps = n_pages // pages

    def page_spec(i):
        return pl.BlockSpec((None, None, heads, hd, page),
                            lambda bb, s, pt: (layer, pt[bb, (steps - 1 - s) * pages + i], 0, 0, 0))

    const = lambda shape: pl.BlockSpec(shape, lambda bb, s, pt: (0,) * len(shape))
    kern = functools.partial(_sb_decode_kernel, pages=pages, scale=scale)
    return pl.pallas_call(
        kern,
        grid_spec=pltpu.PrefetchScalarGridSpec(
            num_scalar_prefetch=1, grid=(b, steps),
            in_specs=[pl.BlockSpec((1, 1, d), lambda bb, s, pt: (bb, 0, 0)),
                      const(bias_rep.shape), const(tri.shape)]
                     + [page_spec(i) for i in range(pages)] * 2,
            out_specs=pl.BlockSpec((1, heads, hd), lambda bb, s, pt: (bb, 0, 0)),
            scratch_shapes=[pltpu.VMEM((heads, hd, page), F32), pltpu.VMEM((heads, page), F32),
                            pltpu.VMEM((heads, hd, page), F32)]),
        out_shape=jax.ShapeDtypeStruct((b, heads, hd), F32),
        compiler_params=_cparams("arbitrary", "arbitrary"), name="sb_decode",
    )(page_table, q.reshape(b, 1, d), bias_rep, tri, *([pool_kt] * pages), *([pool_vt] * pages))


def _ml_chunk_kernel(q_ref, kt_ref, v_ref, og_ref, gt_ref, bg_ref, hg_ref, tril_ref,
                     o_ref, cx_out_ref, m_out_ref, cx_ref, m_ref, *, heads, dv):
    c = pl.program_id(1)
    tl = gt_ref.shape[0]

    @pl.when(c == 0)
    def _():
        cx_ref[...] = jnp.zeros_like(cx_ref)
        m_ref[...] = jnp.zeros_like(m_ref)

    gts = gt_ref[...] + bg_ref[...]
    lf = -_softplus(-gts)
    bcum = _dot_split_rhs(tril_ref[...], lf, 3)
    bcum = pltpu.roll(bcum, shift=128 - heads, axis=1)
    u_t = (gts - bcum).T
    row = lax.broadcasted_iota(jnp.int32, (tl, tl), 0)
    col = lax.broadcasted_iota(jnp.int32, (tl, tl), 1)
    causal = col <= row
    one_col = (lax.broadcasted_iota(jnp.int32, (tl, dv), 1) == 0).astype(F32)

    hs = range(heads)
    bcol = [bcum[:, h:h + 1] for h in hs]
    m_prev = [m_ref[h, 0:1, 0:1] for h in hs]
    cx = [cx_ref[h] for h in hs]
    umat = [jnp.where(causal, u_t[h:h + 1, :], -jnp.inf) for h in hs]
    cm = [jnp.maximum(m_prev[h], jnp.max(umat[h], axis=1, keepdims=True)) for h in hs]
    sqk = [_dot(q_ref[h], kt_ref[h, 0]) for h in hs]
    qc = [_dot(q_ref[h], cx[h].astype(BF16)) for h in hs]
    w = [jnp.exp(umat[h] - cm[h]) * sqk[h] for h in hs]
    g = [jnp.exp(m_prev[h] - cm[h]) for h in hs]
    m_t = [bcol[h] + cm[h] for h in hs]
    num = [_dot(w[h].astype(BF16), v_ref[h]) + g[h] * qc[h][:, :dv] for h in hs]
    den = [jnp.sum(w[h], axis=1, keepdims=True) + g[h] * qc[h][:, dv:dv + 1] for h in hs]
    hout = [num[h] * (1.0 / jnp.maximum(jnp.abs(den[h]), jnp.exp(-m_t[h]))) for h in hs]
    ms = [jnp.mean(hout[h] * hout[h], axis=1, keepdims=True) for h in hs]
    for h in hs:
        hn = hout[h] * lax.rsqrt(ms[h] + RMS_EPS) * hg_ref[h]
        o_ref[:, h * dv:(h + 1) * dv] = (_sigmoid(og_ref[:, h * dv:(h + 1) * dv]) * hn).astype(BF16)
    for h in hs:
        m_new = m_t[h][tl - 1:tl, :]
        b_end = bcol[h][tl - 1:tl, :]
        wk = jnp.exp(b_end - bcol[h] + gts[:, h:h + 1] - m_new)
        g_end = jnp.exp(b_end + m_prev[h] - m_new)
        vx = (jnp.concatenate([v_ref[h].astype(F32), one_col], axis=1) * wk).astype(BF16)
        cx_ref[h] = g_end * cx[h] + _dot(kt_ref[h, 0], vx)
        m_ref[h] = jnp.broadcast_to(m_new, m_ref.shape[1:])

    @pl.when(c == pl.num_programs(1) - 1)
    def _():
        cx_out_ref[0] = cx_ref[...]
        m_out_ref[0] = m_ref[...]


def _ml_chunk(qb, ktb, vb, og, gates, b_gates_row, head_g, tril, batch):
    heads, m, dqk = qb.shape
    dv = vb.shape[2]
    tl = ktb.shape[3]
    nc = m // batch // tl
    ng = gates.shape[1]
    kern = functools.partial(_ml_chunk_kernel, heads=heads, dv=dv)
    return pl.pallas_call(
        kern, grid=(batch, nc),
        in_specs=[pl.BlockSpec((heads, tl, dqk), lambda b, c: (0, b * nc + c, 0)),
                  pl.BlockSpec((heads, 1, dqk, tl), lambda b, c: (0, b * nc + c, 0, 0)),
                  pl.BlockSpec((heads, tl, dv), lambda b, c: (0, b * nc + c, 0)),
                  pl.BlockSpec((tl, heads * dv), lambda b, c: (b * nc + c, 0)),
                  pl.BlockSpec((tl, ng), lambda b, c: (b * nc + c, 0)),
                  pl.BlockSpec((1, ng), lambda b, c: (0, 0)),
                  pl.BlockSpec((heads, 1, dv), lambda b, c: (0, 0, 0)),
                  pl.BlockSpec((tl, tl), lambda b, c: (0, 0))],
        out_specs=[pl.BlockSpec((tl, heads * dv), lambda b, c: (b * nc + c, 0)),
                   pl.BlockSpec((1, heads, dqk, 2 * dv), lambda b, c: (b, 0, 0, 0)),
                   pl.BlockSpec((1, heads, 8, 128), lambda b, c: (b, 0, 0, 0))],
        out_shape=[jax.ShapeDtypeStruct((m, heads * dv), BF16),
                   jax.ShapeDtypeStruct((batch, heads, dqk, 2 * dv), F32),
                   jax.ShapeDtypeStruct((batch, heads, 8, 128), F32)],
        scratch_shapes=[pltpu.VMEM((heads, dqk, 2 * dv), F32), pltpu.VMEM((heads, 8, 128), F32)],
        compiler_params=_cparams("arbitrary", "arbitrary"), name="ml_chunk",
    )(qb, ktb, vb, og, gates, b_gates_row, head_g, tril)


def _ml_step_kernel(pr_ref, c0_ref, n0_ref, m0_ref, bg_ref, hg_ref,
                    o_ref, c_ref, n_ref, m_ref, *, heads, dqk, dv):
    hq, hv = heads * dqk, heads * dv
    gts = pr_ref[0, :, 2 * hq + 2 * hv:] + bg_ref[...]
    ig = gts[:, 0:heads]
    lf = -_softplus(-gts[:, heads:2 * heads])
    inter = lf + m0_ref[0]
    m_t = jnp.maximum(inter, ig)
    m_ref[0] = m_t
    wgt = jnp.exp(ig - m_t)
    g = jnp.exp(inter - m_t)
    floor = jnp.exp(-m_t)
    eye = (lax.broadcasted_iota(jnp.int32, (dqk, dqk), 0)
           == lax.broadcasted_iota(jnp.int32, (dqk, dqk), 1))

    hs = range(heads)
    q = [pr_ref[0, :, h * dqk:(h + 1) * dqk] * (dqk ** -0.5) for h in hs]
    k = [pr_ref[0, :, hq + h * dqk:hq + (h + 1) * dqk] for h in hs]
    v = [pr_ref[0, :, 2 * hq + h * dv:2 * hq + (h + 1) * dv] for h in hs]
    n0 = [n0_ref[0, h:h + 1, :] for h in hs]
    qk = [jnp.sum(q[h] * k[h], axis=1, keepdims=True) for h in hs]
    qn = [jnp.sum(q[h] * n0[h], axis=1, keepdims=True) for h in hs]
    qc = [_dot(q[h].astype(BF16), c0_ref[0, h].astype(BF16)) for h in hs]
    kv = [_dot(jnp.where(eye, wgt[:, h:h + 1] * k[h], 0.0).astype(BF16),
               jnp.broadcast_to(v[h], (dqk, dv)).astype(BF16)) for h in hs]
    hout = []
    for h in hs:
        w_h, g_h = wgt[:, h:h + 1], g[:, h:h + 1]
        num = (w_h * qk[h]) * v[h] + g_h * qc[h]
        den = w_h * qk[h] + g_h * qn[h]
        hout.append(num * (1.0 / jnp.maximum(jnp.abs(den), floor[:, h:h + 1])))
        c_ref[0, h] = g_h * c0_ref[0, h] + kv[h]
        n_ref[0, h:h + 1, :] = g_h * n0[h] + w_h * k[h]
    ms = [jnp.mean(hout[h] * hout[h], axis=1, keepdims=True) for h in hs]
    for h in hs:
        og = pr_ref[0, :, 2 * hq + hv + h * dv:2 * hq + hv + (h + 1) * dv]
        hn = hout[h] * lax.rsqrt(ms[h] + RMS_EPS) * hg_ref[h]
        o_ref[0, :, h * dv:(h + 1) * dv] = _sigmoid(og) * hn


def _ml_step(proj, c0, n0, m0, b_gates_row, head_g, heads, dqk, dv):
    b, n = proj.shape
    hv = heads * dv
    kern = functools.partial(_ml_step_kernel, heads=heads, dqk=dqk, dv=dv)
    out, c, nn, mm = pl.pallas_call(
        kern, grid=(b,),
        in_specs=[pl.BlockSpec((1, 1, n), lambda i: (i, 0, 0)),
                  pl.BlockSpec((1, heads, dqk, dv), lambda i: (i, 0, 0, 0)),
                  pl.BlockSpec((1, heads, dqk), lambda i: (i, 0, 0)),
                  pl.BlockSpec((1, 1, heads), lambda i: (i, 0, 0)),
                  pl.BlockSpec((1, b_gates_row.shape[1]), lambda i: (0, 0)),
                  pl.BlockSpec((heads, 1, dv), lambda i: (0, 0, 0))],
        out_specs=[pl.BlockSpec((1, 1, hv), lambda i: (i, 0, 0)),
                   pl.BlockSpec((1, heads, dqk, dv), lambda i: (i, 0, 0, 0)),
                   pl.BlockSpec((1, heads, dqk), lambda i: (i, 0, 0)),
                   pl.BlockSpec((1, 1, heads), lambda i: (i, 0, 0))],
        out_shape=[jax.ShapeDtypeStruct((b, 1, hv), F32),
                   jax.ShapeDtypeStruct((b, heads, dqk, dv), F32),
                   jax.ShapeDtypeStruct((b, heads, dqk), F32),
                   jax.ShapeDtypeStruct((b, 1, heads), F32)],
        compiler_params=_cparams("arbitrary"), name="ml_step",
    )(proj.reshape(b, 1, n), c0, n0, m0.reshape(b, 1, heads), b_gates_row, head_g)
    return out.reshape(b, hv), c, nn, mm.reshape(b, heads)


def _suffix_ones(n):
    i = lax.broadcasted_iota(jnp.int32, (n, n), 0)
    j = lax.broadcasted_iota(jnp.int32, (n, n), 1)
    return (i >= j).astype(BF16)


def kernel(x_prompt, x_sample, cache_k, cache_v, state_C, state_n, state_m, page_table, p_prompt, p_sample, norm_g, ffn_w_in, ffn_w_out, sb_w_qkv, sb_w_o, sb_logit_bias, ml_w_in, ml_b_gates, ml_head_g, ml_w_out, ple_w_proj, ple_w_gate, final_norm_g):
    batch, seq, d = x_prompt.shape
    dec_batch = x_sample.shape[0]
    depth = norm_g.shape[0]
    sb_heads = sb_logit_bias.shape[1]
    hd = d // sb_heads
    ml_heads, dv = ml_head_g.shape[1], ml_head_g.shape[2]
    dqk = (ml_w_in.shape[2] - 2 * ml_heads * dv - 2 * ml_heads) // (2 * ml_heads)
    page = cache_k.shape[2]
    mp = batch * seq

    xp = x_prompt.reshape(mp, d)
    xs = x_sample.reshape(dec_batch, d)
    pp = p_prompt.reshape(depth, mp, -1)
    ps = p_sample.reshape(depth, dec_batch, -1)
    pool_kt = jnp.transpose(cache_k, (0, 1, 3, 4, 2))
    pool_vt = jnp.transpose(cache_v, (0, 1, 3, 4, 2))

    sb_tile = min(SB_TILE, seq)
    ml_tile = min(ML_TILE, seq)
    tri_sb = _suffix_ones(sb_tile)
    tril_ml = _suffix_ones(ml_tile)
    tri_dec = _suffix_ones(page)

    gate_pad = 128 - 2 * ml_heads
    norm_rows = norm_g.reshape(depth, 4, 1, d)
    final_row = final_norm_g.reshape(1, d)

    w_ffn_in, w_ffn_out = ffn_w_in.astype(BF16), ffn_w_out.astype(BF16)
    w_sb_qkv, w_sb_o = sb_w_qkv.astype(BF16), sb_w_o.astype(BF16)
    w_ml_in = jnp.pad(ml_w_in, ((0, 0), (0, 0), (0, gate_pad))).astype(BF16)
    w_ml_o = ml_w_out.astype(BF16)
    w_ple_gate, w_ple_proj = ple_w_gate.astype(BF16), ple_w_proj.astype(BF16)

    kv_prompt = None
    ks_l, vs_l = [], []
    cp_l, np_l, mp_l, cs_l, ns_l, ms_l = [], [], [], [], [], []
    for i in range(depth):
        g = norm_rows[i]
        j = i // 2
        xp = _ffn(xp, g[0], _layer(w_ffn_in, i, 0), _layer(w_ffn_out, i, 0))
        xs = _ffn(xs, g[0], _layer(w_ffn_in, i, 0), _layer(w_ffn_out, i, 0))
        if i % 2 == 0:
            w_qkv, w_o = _layer(w_sb_qkv, j), _layer(w_sb_o, j)
            kf, vf, qb, ktb, vb = _sb_proj(xp, g[1], w_qkv, sb_heads, sb_tile, batch, kv_prompt)
            kv_prompt = (kf, vf)
            op = _sb_attn(qb, ktb, vb, sb_logit_bias[j], tri_sb, batch)
            qkv_s = _rms_matmul(xs, g[1], w_qkv)
            ks_l.append(qkv_s[:, d:2 * d].reshape(dec_batch, 1, sb_heads, hd))
            vs_l.append(qkv_s[:, 2 * d:].reshape(dec_batch, 1, sb_heads, hd))
            bias_rep = jnp.broadcast_to(sb_logit_bias[j][:, None], (sb_heads, page))
            os_ = _sb_decode(qkv_s[:, :d], pool_kt, pool_vt, j, page_table, bias_rep,
                             tri_dec, hd ** -0.5).reshape(dec_batch, d)
        else:
            w_in, w_o = _layer(w_ml_in, j), _layer(w_ml_o, j)
            bg_row = jnp.pad(ml_b_gates[j], (0, gate_pad)).reshape(1, 128)
            hg = ml_head_g[j].reshape(ml_heads, 1, dv)
            og, gates, qb, ktb, vb = _ml_proj(xp, g[1], w_in, ml_heads, dqk, dv, ml_tile)
            op, cx, mm = _ml_chunk(qb, ktb, vb, og, gates, bg_row, hg, tril_ml, batch)
            cp_l.append(cx[..., :dv])
            np_l.append(cx[..., dv])
            mp_l.append(mm[:, :, 0, 0])
            proj_s = _rms_matmul(xs, g[1], w_in)
            os_, c_s, n_s, m_s = _ml_step(proj_s, state_C[j], state_n[j], state_m[j], bg_row, hg,
                                          ml_heads, dqk, dv)
            cs_l.append(c_s)
            ns_l.append(n_s)
            ms_l.append(m_s)
        g_final = final_row if i == depth - 1 else None
        post = (g[2], g[3], w_o, _layer(w_ffn_in, i, 1), _layer(w_ffn_out, i, 1),
                _layer(w_ple_gate, i), _layer(w_ple_proj, i), g_final)
        xp = _post_mixer(op, xp, (pp, i), *post)
        xs = _post_mixer(os_, xs, (ps, i), *post)
    kp, vp = (jnp.transpose(a, (0, 1, 4, 2, 3)) for a in kv_prompt)
    return (xp.reshape(batch, seq, d), xs.reshape(dec_batch, 1, d),
            kp, vp, jnp.stack(cp_l), jnp.stack(np_l), jnp.stack(mp_l),
            jnp.stack(ks_l), jnp.stack(vs_l), jnp.stack(cs_l), jnp.stack(ns_l), jnp.stack(ms_l))
```
